```python
import jax, jax.numpy as jnp
from jax import lax
import numpy as np

D_MODEL = 4096
BATCH = 1
SEQ = 16384
DEPTH = 4

N_MIXERS = 2
N_GLA_LAYERS = (DEPTH + 1) // 2
N_SGU_LAYERS = DEPTH // 2

GLA_HEADS = 8
GLA_DK_TOTAL = D_MODEL // 2
GLA_DV_TOTAL = D_MODEL
GLA_HEAD_K = GLA_DK_TOTAL // GLA_HEADS
GLA_HEAD_V = GLA_DV_TOTAL // GLA_HEADS
GLA_GATE_RANK = 16
GLA_TAU = 16.0
GLA_CHUNK = 64
GLA_SPLITS = (GLA_DK_TOTAL, 2 * GLA_DK_TOTAL, 2 * GLA_DK_TOTAL + GLA_DV_TOTAL,
              2 * GLA_DK_TOTAL + 2 * GLA_DV_TOTAL, 2 * GLA_DK_TOTAL + 2 * GLA_DV_TOTAL + GLA_GATE_RANK)
GLA_IN_COLS = 2 * GLA_DK_TOTAL + 2 * GLA_DV_TOTAL + 2 * GLA_GATE_RANK

SGU_HALF = D_MODEL
SGU_GROUPS = 8
SGU_GROUP_W = SGU_HALF // SGU_GROUPS
SGU_CHUNK = 128

N_EXPERTS = 16
N_GROUPS = 4
EXPERTS_PER_GROUP = N_EXPERTS // N_GROUPS
TOP_K = 2
D_EXPERT = 768
DISPATCH_BLOCK = 256

DEEPNORM_ALPHA = (2 * DEPTH) ** 0.25
DEEPNORM_BETA = (8 * DEPTH) ** -0.25
LN_EPS = 1e-5
RMS_EPS = 1e-6

kernel_name = 'hybrid_gla_sgu_grouped_moe_encoder'


def layer_norm(x, g, b):
    xf = x.astype(jnp.float32)
    mu = jnp.mean(xf, axis=-1, keepdims=True)
    xc = xf - mu
    var = jnp.mean(xc * xc, axis=-1, keepdims=True)
    return (xc * lax.rsqrt(var + LN_EPS) * g + b).astype(x.dtype)


def gla_chunked(q, k, v, g, strict):
    B, H, L, dk = q.shape
    dv = v.shape[-1]
    C = GLA_CHUNK
    N = L // C
    q, k, g = (t.reshape(B, H, N, C, dk) for t in (q, k, g))
    v = v.reshape(B, H, N, C, dv)
    b = jnp.cumsum(g, axis=3)
    b_last = b[:, :, :, -1:, :]
    q_dec = q * jnp.exp(b)
    k_dec = k * jnp.exp(-b)
    scores = jnp.einsum('bhnid,bhnjd->bhnij', q_dec, k_dec)
    mask = jnp.tril(jnp.ones((C, C), dtype=bool), -1 if strict else 0)
    o_intra = jnp.einsum('bhnij,bhnjv->bhniv', jnp.where(mask, scores, 0.0), v)
    k_state = k * jnp.exp(b_last - b)
    chunk_decay = jnp.exp(b_last[:, :, :, 0, :])

    def step(S, xs):
        q_n, k_n, v_n, a_n = xs
        o_n = jnp.einsum('bhid,bhdv->bhiv', q_n, S)
        S = a_n[..., None] * S + jnp.einsum('bhjd,bhjv->bhdv', k_n, v_n)
        return S, o_n

    S0 = jnp.zeros((B, H, dk, dv), jnp.float32)
    xs = tuple(jnp.moveaxis(t, 2, 0) for t in (q_dec, k_state, v, chunk_decay))
    _, o_inter = lax.scan(step, S0, xs)
    o = o_intra + jnp.moveaxis(o_inter, 0, 2)
    return o.reshape(B, H, L, dv)


def gla_mixer(x, w_in, w_gate2, b_gate, norm_g, w_out):
    B, L, _ = x.shape
    proj = x @ w_in
    q, k, v, r, z_f, z_b = jnp.split(proj, GLA_SPLITS, axis=-1)

    def heads(t, d):
        return t.reshape(B, L, GLA_HEADS, d).transpose(0, 2, 1, 3).astype(jnp.float32)

    q = heads(q, GLA_HEAD_K) * (GLA_HEAD_K ** -0.5)
    k = heads(k, GLA_HEAD_K)
    v = heads(v, GLA_HEAD_V)
    g_f = heads(jax.nn.log_sigmoid((z_f @ w_gate2[0] + b_gate[0]).astype(jnp.float32)) / GLA_TAU, GLA_HEAD_K)
    g_b = heads(jax.nn.log_sigmoid((z_b @ w_gate2[1] + b_gate[1]).astype(jnp.float32)) / GLA_TAU, GLA_HEAD_K)
    o_f = gla_chunked(q, k, v, g_f, False)
    flip = lambda t: jnp.flip(t, axis=2)
    o_b = flip(gla_chunked(flip(q), flip(k), flip(v), flip(g_b), True))
    o = o_f + o_b
    o = o * lax.rsqrt(jnp.mean(o * o, axis=-1, keepdims=True) + RMS_EPS)
    o = o.transpose(0, 2, 1, 3).reshape(B, L, GLA_DV_TOTAL) * norm_g
    o = o * jax.nn.silu(r.astype(jnp.float32))
    return o.astype(x.dtype) @ w_out


def sgu_mixer(x, w_in, ln_g, ln_b, w_s, b_s, w_out):
    B, L, _ = x.shape
    z = jax.nn.gelu(x @ w_in, approximate=False)
    u, v = jnp.split(z, 2, axis=-1)
    v = layer_norm(v, ln_g, ln_b)
    v = v.reshape(B, L // SGU_CHUNK, SGU_CHUNK, SGU_GROUPS, SGU_GROUP_W)
    v = jnp.einsum('gij,bnjgc->bnigc', w_s, v) + jnp.transpose(b_s)[:, :, None]
    v = v.reshape(B, L, SGU_HALF)
    return (u * v) @ w_out


def route(h, router_w, router_b):
    T = h.shape[0]
    scores = jax.nn.sigmoid((h @ router_w).astype(jnp.float32))
    sel = (scores + router_b.astype(jnp.float32)).reshape(T, N_GROUPS, EXPERTS_PER_GROUP)
    group_score = jnp.sum(lax.top_k(sel, TOP_K)[0], axis=-1)
    g_star = jnp.argmax(group_score, axis=-1)
    sel_in = jnp.take_along_axis(sel, g_star[:, None, None], axis=1)[:, 0]
    _, loc = lax.top_k(sel_in, TOP_K)
    expert = g_star[:, None] * EXPERTS_PER_GROUP + loc
    s_sel = jnp.take_along_axis(scores, expert, axis=1)
    gate = s_sel / jnp.sum(s_sel, axis=-1, keepdims=True)
    return expert, gate


def moe(h, router_w, router_b, w_gate, w_up, w_down):
    B, L, D = h.shape
    T = B * L
    hf = h.reshape(T, D)
    expert, gate = route(hf, router_w, router_b)
    N = T * TOP_K
    BLK = DISPATCH_BLOCK
    e_flat = expert.reshape(N)
    tok_flat = jnp.arange(N, dtype=jnp.int32) // TOP_K
    w_flat = gate.reshape(N).astype(h.dtype)
    order = jnp.argsort(e_flat)
    e_sorted = e_flat[order]
    counts = jnp.bincount(e_flat, length=N_EXPERTS)
    starts = jnp.cumsum(counts) - counts
    padded = (counts + BLK - 1) // BLK * BLK
    pends = jnp.cumsum(padded)
    pstarts = pends - padded
    dest = pstarts[e_sorted] + jnp.arange(N, dtype=jnp.int32) - starts[e_sorted]
    NB = (N + BLK - 1) // BLK + N_EXPERTS
    P = NB * BLK
    row_token = jnp.full((P,), T, dtype=jnp.int32).at[dest].set(tok_flat[order])
    row_gate = jnp.zeros((P,), h.dtype).at[dest].set(w_flat[order])
    block_expert = jnp.minimum(jnp.searchsorted(pends, jnp.arange(NB) * BLK, side='right'), N_EXPERTS - 1)
    h_pad = jnp.concatenate([hf, jnp.zeros((1, D), h.dtype)], axis=0)

    def expert_block(args):
        tok, gw, e = args
        xb = h_pad[tok]
        a = jax.nn.silu(xb @ w_gate[e]) * (xb @ w_up[e])
        return (a @ w_down[e]) * gw[:, None]

    out = lax.map(expert_block, (row_token.reshape(NB, BLK), row_gate.reshape(NB, BLK), block_expert))
    y = jnp.zeros((T + 1, D), h.dtype).at[row_token].add(out.reshape(P, D))
    return y[:T].reshape(B, L, D)


def setup_inputs(seed: int = 0) -> dict:
    key = jax.random.key(seed)
    ks = jax.random.split(key, 24)
    nrm = jax.random.normal
    f32 = jnp.float32
    D, NG, NM = D_MODEL, N_GLA_LAYERS, N_SGU_LAYERS
    s_in = D ** -0.5
    x = nrm(ks[0], (BATCH, SEQ, D), f32)
    gla_w_in = jnp.concatenate([
        nrm(ks[1], (NG, D, 2 * GLA_DK_TOTAL), f32) * s_in,
        nrm(ks[2], (NG, D, GLA_DV_TOTAL), f32) * (s_in * DEEPNORM_BETA),
        nrm(ks[3], (NG, D, GLA_DV_TOTAL), f32) * s_in,
        nrm(ks[4], (NG, D, 2 * GLA_GATE_RANK), f32) * s_in], axis=-1)
    gla_w_gate2 = nrm(ks[5], (NG, 2, GLA_GATE_RANK, GLA_DK_TOTAL), f32) * GLA_GATE_RANK ** -0.5
    gla_b_gate = 0.1 * nrm(ks[6], (NG, 2, GLA_DK_TOTAL), f32)
    gla_norm_g = 1.0 + 0.02 * nrm(ks[7], (NG, GLA_DV_TOTAL), f32)
    gla_w_out = nrm(ks[8], (NG, GLA_DV_TOTAL, D), f32) * (GLA_DV_TOTAL ** -0.5 * DEEPNORM_BETA)
    sgu_w_in = nrm(ks[9], (NM, D, 2 * SGU_HALF), f32) * s_in
    sgu_ln_g = 1.0 + 0.02 * nrm(ks[10], (NM, SGU_HALF), f32)
    sgu_ln_b = 0.02 * nrm(ks[11], (NM, SGU_HALF), f32)
    sgu_w_s = nrm(ks[12], (NM, SGU_GROUPS, SGU_CHUNK, SGU_CHUNK), f32) * SGU_CHUNK ** -0.5
    sgu_b_s = 1.0 + 0.02 * nrm(ks[13], (NM, SGU_GROUPS, SGU_CHUNK), f32)
    sgu_w_out = nrm(ks[14], (NM, SGU_HALF, D), f32) * (SGU_HALF ** -0.5 * DEEPNORM_BETA)
    router_w = nrm(ks[15], (D, N_EXPERTS), f32) * s_in
    router_b = 0.01 * nrm(ks[16], (N_EXPERTS,), f32)
    moe_w_gate = nrm(ks[17], (DEPTH, N_EXPERTS, D, D_EXPERT), f32) * s_in
    moe_w_up = nrm(ks[18], (DEPTH, N_EXPERTS, D, D_EXPERT), f32) * s_in
    moe_w_down = nrm(ks[19], (DEPTH, N_EXPERTS, D_EXPERT, D), f32) * (D_EXPERT ** -0.5 * DEEPNORM_BETA)
    ln_g = 1.0 + 0.02 * nrm(ks[20], (DEPTH, 2, D), f32)
    ln_b = 0.02 * nrm(ks[21], (DEPTH, 2, D), f32)
    return {'x': x, 'gla_w_in': gla_w_in, 'gla_w_gate2': gla_w_gate2, 'gla_b_gate': gla_b_gate,
            'gla_norm_g': gla_norm_g, 'gla_w_out': gla_w_out, 'sgu_w_in': sgu_w_in,
            'sgu_ln_g': sgu_ln_g, 'sgu_ln_b': sgu_ln_b, 'sgu_w_s': sgu_w_s, 'sgu_b_s': sgu_b_s,
            'sgu_w_out': sgu_w_out, 'router_w': router_w, 'router_b': router_b,
            'moe_w_gate': moe_w_gate, 'moe_w_up': moe_w_up, 'moe_w_down': moe_w_down,
            'ln_g': ln_g, 'ln_b': ln_b}


def reference(x, gla_w_in, gla_w_gate2, gla_b_gate, gla_norm_g, gla_w_out, sgu_w_in,
              sgu_ln_g, sgu_ln_b, sgu_w_s, sgu_b_s, sgu_w_out, router_w, router_b,
              moe_w_gate, moe_w_up, moe_w_down, ln_g, ln_b):
    for i in range(DEPTH):
        j = i // N_MIXERS
        if i % N_MIXERS == 0:
            h = gla_mixer(x, gla_w_in[j], gla_w_gate2[j], gla_b_gate[j], gla_norm_g[j], gla_w_out[j])
        else:
            h = sgu_mixer(x, sgu_w_in[j], sgu_ln_g[j], sgu_ln_b[j], sgu_w_s[j], sgu_b_s[j], sgu_w_out[j])
        x = layer_norm(DEEPNORM_ALPHA * x + h, ln_g[i, 0], ln_b[i, 0])
        m = moe(x, router_w, router_b, moe_w_gate[i], moe_w_up[i], moe_w_down[i])
        x = layer_norm(DEEPNORM_ALPHA * x + m, ln_g[i, 1], ln_b[i, 1])
    return x
```

```python
import functools

import jax
import jax.numpy as jnp
from jax import lax
from jax.experimental import pallas as pl
from jax.experimental.pallas import tpu as pltpu

GLA_HEADS = 8
GLA_GATE_RANK = 16
GLA_TAU = 16.0
GLA_CHUNK = 64
SGU_GROUPS = 8
SGU_CHUNK = 128
N_EXPERTS = 16
N_GROUPS = 4
EXPERTS_PER_GROUP = N_EXPERTS // N_GROUPS
TOP_K = 2
LN_EPS = 1e-5
RMS_EPS = 1e-6

LANES = 128
SUBLANES = 8
VMEM_LIMIT_BYTES = 56 * 1024 * 1024

DISPATCH_BLOCK = 256
NEG_BIG = -1e30

F32 = jnp.float32
BF16 = jnp.bfloat16
_NT = (((1,), (1,)), ((), ()))
_TN = (((0,), (0,)), ((), ()))


def _params(*sem):
    return pltpu.CompilerParams(dimension_semantics=sem, vmem_limit_bytes=VMEM_LIMIT_BYTES)


def _layer_norm_rows(y, g, b):
    mu = jnp.mean(y, axis=-1, keepdims=True)
    yc = y - mu
    var = jnp.mean(yc * yc, axis=-1, keepdims=True)
    return yc * lax.rsqrt(var + LN_EPS) * g + b


def _mm_kernel(x_ref, w_ref, o_ref, *, act):
    acc = jnp.dot(x_ref[...], w_ref[...], preferred_element_type=F32)
    if act == "gelu":
        acc = 0.5 * acc * (1.0 + lax.erf(acc * (2.0 ** -0.5)))
    o_ref[...] = acc.astype(o_ref.dtype)


def _tile(dim, target, align):
    t = min(target, dim) // align * align
    while dim % t:
        t -= align
    return t


def _matmul(x, w, out_dtype, act=None, tm=1024, tn=1024):
    m, k = x.shape
    n = w.shape[1]
    tm, tn = _tile(m, tm, SUBLANES), _tile(n, tn, LANES)
    return pl.pallas_call(
        functools.partial(_mm_kernel, act=act),
        grid=(m // tm, n // tn),
        in_specs=[pl.BlockSpec((tm, k), lambda i, j: (i, 0)),
                  pl.BlockSpec((k, tn), lambda i, j: (0, j))],
        out_specs=pl.BlockSpec((tm, tn), lambda i, j: (i, j)),
        out_shape=jax.ShapeDtypeStruct((m, n), out_dtype),
        compiler_params=_params("parallel", "arbitrary"),
        name="dense_matmul",
    )(x, w)


LN_ROWS = 32


def _mm_ln_kernel(a_ref, w_ref, res_ref, g_ref, b_ref, of_ref, ob_ref, *, alpha, nj, tm, tn):
    j = pl.program_id(1)
    y = alpha * res_ref[...] + jnp.dot(a_ref[...], w_ref[...], preferred_element_type=F32)
    for jj in range(nj):
        @pl.when(j == jj)
        def _():
            of_ref[:, jj * tn:(jj + 1) * tn] = y

    @pl.when(j == nj - 1)
    def _():
        g = g_ref[...]
        b = b_ref[...]

        def body(r, carry):
            rows = pl.ds(pl.multiple_of(r * LN_ROWS, LN_ROWS), LN_ROWS)
            out = _layer_norm_rows(of_ref[rows, :], g, b)
            of_ref[rows, :] = out
            ob_ref[rows, :] = out.astype(BF16)
            return carry

        lax.fori_loop(0, tm // LN_ROWS, body, 0)


def _matmul_res_ln(a, w, res, g, b, alpha, tm=512, tn=512):
    m, kdim = a.shape
    n = w.shape[1]
    tm, tn = _tile(m, tm, SUBLANES), _tile(n, tn, LANES)
    nj = n // tn
    return pl.pallas_call(
        functools.partial(_mm_ln_kernel, alpha=alpha, nj=nj, tm=tm, tn=tn),
        grid=(m // tm, nj),
        in_specs=[pl.BlockSpec((tm, kdim), lambda i, j: (i, 0)),
                  pl.BlockSpec((kdim, tn), lambda i, j: (0, j)),
                  pl.BlockSpec((tm, tn), lambda i, j: (i, j)),
                  pl.BlockSpec((1, n), lambda i, j: (0, 0)),
                  pl.BlockSpec((1, n), lambda i, j: (0, 0))],
        out_specs=[pl.BlockSpec((tm, n), lambda i, j: (i, 0)),
                   pl.BlockSpec((tm, n), lambda i, j: (i, 0))],
        out_shape=[jax.ShapeDtypeStruct((m, n), F32),
                   jax.ShapeDtypeStruct((m, n), BF16)],
        compiler_params=_params("parallel", "arbitrary"),
        name="matmul_residual_layernorm",
    )(a, w, res, g.reshape(1, n), b.reshape(1, n))


def _log_sigmoid(x):
    return jnp.minimum(x, 0.0) - jnp.log1p(jnp.exp(-jnp.abs(x)))


def _gla_kernel(*refs, reverse, finalize, n_chunks, scale):
    if finalize:
        (q_ref, k_ref, v_ref, z_ref, wg_ref, bg_ref, oprev_ref, r_ref, ng_ref,
         o_ref, g_scr, st_scr) = refs
    else:
        q_ref, k_ref, v_ref, z_ref, wg_ref, bg_ref, o_ref, g_scr, st_scr = refs
    C = GLA_CHUNK

    @pl.when(pl.program_id(1) == 0)
    def _():
        st_scr[...] = jnp.zeros_like(st_scr)

    logit = jnp.dot(z_ref[...], wg_ref[...], preferred_element_type=F32,
                    precision=lax.Precision.HIGHEST) + bg_ref[...]
    g_scr[...] = _log_sigmoid(logit) * (1.0 / GLA_TAU)

    row = lax.broadcasted_iota(jnp.int32, (C, C), 0)
    col = lax.broadcasted_iota(jnp.int32, (C, C), 1)
    if reverse:
        tri = (col >= row).astype(F32)
        mask = col > row
        last = 0
    else:
        tri = (col <= row).astype(F32)
        mask = col <= row
        last = C - 1

    def body(c, carry):
        cc = (n_chunks - 1 - c) if reverse else c
        rows = pl.ds(pl.multiple_of(cc * C, C), C)
        b = jnp.dot(tri, g_scr[rows, :], preferred_element_type=F32,
                    precision=lax.Precision.HIGHEST)
        b_last = b[last:last + 1, :]
        q = q_ref[rows, :].astype(F32) * scale
        k = k_ref[rows, :].astype(F32)
        v = v_ref[rows, :]
        q_dec = (q * jnp.exp(b)).astype(BF16)
        k_dec = (k * jnp.exp(-b)).astype(BF16)
        k_state = (k * jnp.exp(b_last - b)).astype(BF16)
        decay = jnp.exp(b_last)
        scores = lax.dot_general(q_dec, k_dec, _NT, preferred_element_type=F32)
        scores = jnp.where(mask, scores, 0.0).astype(BF16)
        st = st_scr[...]
        o = jnp.dot(scores, v, preferred_element_type=F32)
        o = o + lax.dot_general(q_dec, st.astype(BF16), _NT, preferred_element_type=F32)
        st_scr[...] = decay * st + lax.dot_general(v, k_state, _TN, preferred_element_type=F32)
        if finalize:
            o = o + oprev_ref[rows, :].astype(F32)
            o = o * lax.rsqrt(jnp.mean(o * o, axis=-1, keepdims=True) + RMS_EPS)
            r = r_ref[rows, :].astype(F32)
            o = o * ng_ref[...] * (r * jax.nn.sigmoid(r))
        o_ref[rows, :] = o.astype(o_ref.dtype)
        return carry

    lax.fori_loop(0, n_chunks, body, 0)


def _gla_direction(proj, z, wg_pad, b_gate, *, reverse, o_prev=None, norm_g=None, tl=512):
    L = proj.shape[0]
    H = GLA_HEADS
    dk_total = wg_pad.shape[1]
    dk = dk_total // H
    dv_total = (proj.shape[1] - 2 * dk_total) // 2
    dv = dv_total // H
    tl = min(tl, L)
    nb = L // tl
    finalize = o_prev is not None
    blk = (lambda i: nb - 1 - i) if reverse else (lambda i: i)
    k_off = dk_total // dk
    v_off = 2 * dk_total // dv
    r_off = v_off + dv_total // dv

    in_specs = [
        pl.BlockSpec((tl, dk), lambda h, i: (blk(i), h)),
        pl.BlockSpec((tl, dk), lambda h, i: (blk(i), k_off + h)),
        pl.BlockSpec((tl, dv), lambda h, i: (blk(i), v_off + h)),
        pl.BlockSpec((tl, LANES), lambda h, i: (blk(i), 0)),
        pl.BlockSpec((LANES, dk), lambda h, i: (0, h)),
        pl.BlockSpec((1, dk), lambda h, i: (0, h)),
    ]
    args = [proj, proj, proj, z, wg_pad, b_gate.reshape(1, dk_total)]
    if finalize:
        in_specs += [
            pl.BlockSpec((tl, dv), lambda h, i: (blk(i), h)),
            pl.BlockSpec((tl, dv), lambda h, i: (blk(i), r_off + h)),
            pl.BlockSpec((1, dv), lambda h, i: (0, h)),
        ]
        args += [o_prev, proj, norm_g.reshape(1, dv_total)]
    return pl.pallas_call(
        functools.partial(_gla_kernel, reverse=reverse, finalize=finalize,
                          n_chunks=tl // GLA_CHUNK, scale=float(dk) ** -0.5),
        grid=(H, nb),
        in_specs=in_specs,
        out_specs=pl.BlockSpec((tl, dv), lambda h, i: (blk(i), h)),
        out_shape=jax.ShapeDtypeStruct((L, dv_total), BF16),
        scratch_shapes=[pltpu.VMEM((tl, dk), F32), pltpu.VMEM((dv, dk), F32)],
        compiler_params=_params("parallel", "arbitrary"),
        name="gla_backward_scan" if reverse else "gla_forward_scan",
    )(*args)


def _gla_mixer(xb, w_in, w_gate2, b_gate, norm_g):
    dk_total = w_gate2.shape[-1]
    n_main = w_in.shape[1] - 2 * GLA_GATE_RANK
    rank = GLA_GATE_RANK
    proj = _matmul(xb, w_in[:, :n_main].astype(BF16), BF16)
    w_z = jnp.zeros((w_in.shape[0], LANES), BF16).at[:, :2 * rank].set(w_in[:, n_main:].astype(BF16))
    z = _matmul(xb, w_z, F32)
    wg_f = jnp.zeros((LANES, dk_total), F32).at[:rank].set(w_gate2[0])
    wg_b = jnp.zeros((LANES, dk_total), F32).at[rank:2 * rank].set(w_gate2[1])
    o_f = _gla_direction(proj, z, wg_f, b_gate[0], reverse=False)
    return _gla_direction(proj, z, wg_b, b_gate[1], reverse=True, o_prev=o_f, norm_g=norm_g)


def _sgu_kernel(u_ref, v_ref, g_ref, b_ref, ws_ref, bias_ref, o_ref, *, n_chunks, gw):
    C = SGU_CHUNK
    for c in range(n_chunks):
        rows = slice(c * C, (c + 1) * C)
        vn = _layer_norm_rows(v_ref[rows, :].astype(F32), g_ref[...], b_ref[...]).astype(BF16)
        for grp in range(SGU_GROUPS):
            cols = slice(grp * gw, (grp + 1) * gw)
            mixed = jnp.dot(ws_ref[grp], vn[:, cols], preferred_element_type=F32) + bias_ref[:, cols]
            o_ref[rows, cols] = (u_ref[rows, cols].astype(F32) * mixed).astype(BF16)


def _sgu_gate(zact, ln_g, ln_b, w_s, b_s, tl=256):
    L = zact.shape[0]
    half = zact.shape[1] // 2
    gw = half // SGU_GROUPS
    tl = min(tl, L)
    bias = jnp.repeat(jnp.transpose(b_s), gw, axis=1)
    return pl.pallas_call(
        functools.partial(_sgu_kernel, n_chunks=tl // SGU_CHUNK, gw=gw),
        grid=(L // tl,),
        in_specs=[pl.BlockSpec((tl, half), lambda i: (i, 0)),
                  pl.BlockSpec((tl, half), lambda i: (i, 1)),
                  pl.BlockSpec((1, half), lambda i: (0, 0)),
                  pl.BlockSpec((1, half), lambda i: (0, 0)),
                  pl.BlockSpec((SGU_GROUPS, SGU_CHUNK, SGU_CHUNK), lambda i: (0, 0, 0)),
                  pl.BlockSpec((SGU_CHUNK, half), lambda i: (0, 0))],
        out_specs=pl.BlockSpec((tl, half), lambda i: (i, 0)),
        out_shape=jax.ShapeDtypeStruct((L, half), BF16),
        compiler_params=_params("parallel"),
        name="sgu_spatial_gate",
    )(zact, zact, ln_g.reshape(1, half), ln_b.reshape(1, half), w_s.astype(BF16), bias)


ROUTER_ROWS = EXPERTS_PER_GROUP * SUBLANES


def _split_bf16(a):
    hi = a.astype(BF16)
    lo = (a - hi.astype(F32)).astype(BF16)
    return hi, lo


def _router_kernel(x_ref, wt_ref, rb_ref, o_ref):
    xh, xl = _split_bf16(x_ref[...])
    wh, wl = _split_bf16(wt_ref[...])
    logits = (lax.dot_general(wh, xh, _NT, preferred_element_type=F32)
              + lax.dot_general(wh, xl, _NT, preferred_element_type=F32)
              + lax.dot_general(wl, xh, _NT, preferred_element_type=F32))
    scores = jax.nn.sigmoid(logits)
    sel = scores + rb_ref[:, 0:1]
    T = logits.shape[1]
    s_m = [scores[SUBLANES * m:SUBLANES * (m + 1), :] for m in range(EXPERTS_PER_GROUP)]
    a_m = [sel[SUBLANES * m:SUBLANES * (m + 1), :] for m in range(EXPERTS_PER_GROUP)]
    hi1, lo1 = jnp.maximum(a_m[0], a_m[1]), jnp.minimum(a_m[0], a_m[1])
    hi2, lo2 = jnp.maximum(a_m[2], a_m[3]), jnp.minimum(a_m[2], a_m[3])
    group_score = jnp.maximum(hi1, hi2) + jnp.maximum(jnp.minimum(hi1, hi2), jnp.maximum(lo1, lo2))
    gid = lax.broadcasted_iota(jnp.int32, (SUBLANES, T), 0)
    gmax = jnp.max(group_score, axis=0, keepdims=True)
    g_star = jnp.min(jnp.where(group_score == gmax, gid, SUBLANES), axis=0, keepdims=True)
    onehot = gid == g_star
    v_m = [jnp.sum(jnp.where(onehot, a, 0.0), axis=0, keepdims=True) for a in a_m]
    r_m = [jnp.sum(jnp.where(onehot, s, 0.0), axis=0, keepdims=True) for s in s_m]
    best1, idx1, raw1 = v_m[0], jnp.zeros_like(g_star), r_m[0]
    for m in range(1, EXPERTS_PER_GROUP):
        take = v_m[m] > best1
        best1 = jnp.where(take, v_m[m], best1)
        idx1 = jnp.where(take, m, idx1)
        raw1 = jnp.where(take, r_m[m], raw1)
    best2 = jnp.full_like(best1, -jnp.inf)
    idx2, raw2 = jnp.zeros_like(g_star), jnp.zeros_like(raw1)
    for m in range(EXPERTS_PER_GROUP):
        take = jnp.logical_and(idx1 != m, v_m[m] > best2)
        best2 = jnp.where(take, v_m[m], best2)
        idx2 = jnp.where(take, m, idx2)
        raw2 = jnp.where(take, r_m[m], raw2)
    denom = raw1 + raw2
    e1 = (g_star * EXPERTS_PER_GROUP + idx1).astype(F32)
    e2 = (g_star * EXPERTS_PER_GROUP + idx2).astype(F32)
    rid = lax.broadcasted_iota(jnp.int32, (SUBLANES, T), 0)
    out = jnp.where(rid == 0, e1, jnp.where(rid == 1, e2, jnp.where(
        rid == 2, raw1 / denom, jnp.where(rid == 3, raw2 / denom, 0.0))))
    o_ref[...] = out


def _route(x, router_w, router_b, tl=512):
    L, D = x.shape
    tl = min(tl, L)
    w_t = jnp.zeros((EXPERTS_PER_GROUP, SUBLANES, D), F32)
    w_t = w_t.at[:, :N_GROUPS].set(router_w.T.reshape(N_GROUPS, EXPERTS_PER_GROUP, D).transpose(1, 0, 2))
    rb = jnp.full((EXPERTS_PER_GROUP, SUBLANES), NEG_BIG, F32)
    rb = rb.at[:, :N_GROUPS].set(router_b.astype(F32).reshape(N_GROUPS, EXPERTS_PER_GROUP).T)
    rb = jnp.broadcast_to(rb.reshape(ROUTER_ROWS, 1), (ROUTER_ROWS, LANES))
    out = pl.pallas_call(
        _router_kernel,
        grid=(L // tl,),
        in_specs=[pl.BlockSpec((tl, D), lambda i: (i, 0)),
                  pl.BlockSpec((ROUTER_ROWS, D), lambda i: (0, 0)),
                  pl.BlockSpec((ROUTER_ROWS, LANES), lambda i: (0, 0))],
        out_specs=pl.BlockSpec((SUBLANES, tl), lambda i: (0, i)),
        out_shape=jax.ShapeDtypeStruct((SUBLANES, L), F32),
        compiler_params=_params("parallel"),
        name="moe_router",
    )(x, w_t.reshape(ROUTER_ROWS, D), rb)
    expert = out[:TOP_K].astype(jnp.int32)
    gate = out[TOP_K:2 * TOP_K]
    return expert, gate


def _rank_kernel(e_ref, rank_ref, cnt_ref, carry_scr):
    @pl.when(jnp.logical_and(pl.program_id(0) == 0, pl.program_id(1) == 0))
    def _():
        carry_scr[...] = jnp.zeros_like(carry_scr)

    e = e_ref[0]
    T = e.shape[1]
    eid = lax.broadcasted_iota(jnp.int32, (N_EXPERTS, T), 0)
    onehot = eid == e
    oh = jnp.where(onehot, 1.0, 0.0)
    s_idx = lax.broadcasted_iota(jnp.int32, (T, T), 0)
    t_idx = lax.broadcasted_iota(jnp.int32, (T, T), 1)
    before = jnp.where(s_idx < t_idx, 1.0, 0.0).astype(BF16)
    prefix = jnp.dot(oh.astype(BF16), before, preferred_element_type=F32)
    carry = carry_scr[:, 0:1]
    rank_ref[0] = jnp.sum(jnp.where(onehot, prefix + carry, 0.0), axis=0, keepdims=True)
    carry_scr[...] = carry_scr[...] + jnp.sum(oh, axis=1, keepdims=True)
    cnt_ref[...] = carry_scr[...]


def _expert_ranks(expert, tl=512):
    K, L = expert.shape
    tl = min(tl, L)
    rank, cnt = pl.pallas_call(
        _rank_kernel,
        grid=(K, L // tl),
        in_specs=[pl.BlockSpec((1, 1, tl), lambda k, i: (k, 0, i))],
        out_specs=[pl.BlockSpec((1, 1, tl), lambda k, i: (k, 0, i)),
                   pl.BlockSpec((N_EXPERTS, LANES), lambda k, i: (0, 0))],
        out_shape=[jax.ShapeDtypeStruct((K, 1, L), F32),
                   jax.ShapeDtypeStruct((N_EXPERTS, LANES), F32)],
        scratch_shapes=[pltpu.VMEM((N_EXPERTS, LANES), F32)],
        compiler_params=_params("arbitrary", "arbitrary"),
        name="moe_expert_rank",
    )(expert.reshape(K, 1, L))
    return rank.reshape(K, L).astype(jnp.int32), cnt[:, 0].astype(jnp.int32)


DISPATCH_TOKENS = 512
DMA_UNROLL = 8


def _dispatch_kernel(dest_ref, pad_lo_ref, pad_hi_ref, x_hbm, xs_hbm, zero_scr, sem, *, n_tokens, tb):
    i = pl.program_id(0)

    def row_copy(t, slot):
        return pltpu.make_async_copy(x_hbm.at[pl.ds(t, 1)], xs_hbm.at[pl.ds(slot, 1)], sem)

    def zero_copy(slot):
        return pltpu.make_async_copy(zero_scr.at[pl.ds(0, 1)], xs_hbm.at[pl.ds(slot, 1)], sem)

    @pl.when(i == 0)
    def _():
        zero_scr[...] = jnp.zeros_like(zero_scr)
        for e in range(N_EXPERTS):
            lo, hi = pad_lo_ref[e], pad_hi_ref[e]
            lax.fori_loop(lo, hi, lambda p, c: (zero_copy(p).start(), c)[1], 0)
            lax.fori_loop(lo, hi, lambda p, c: (zero_copy(p).wait(), c)[1], 0)

    base = i * tb

    def issue(j, carry):
        for u in range(DMA_UNROLL):
            t = base + j * DMA_UNROLL + u
            for k in range(TOP_K):
                row_copy(t, dest_ref[k * n_tokens + t]).start()
        return carry

    lax.fori_loop(0, tb // DMA_UNROLL, issue, 0)

    def drain(j, carry):
        for u in range(DMA_UNROLL * TOP_K):
            row_copy(0, 0).wait()
        return carry

    lax.fori_loop(0, tb // DMA_UNROLL, drain, 0)


def _dispatch(x, dest, pad_lo, pad_hi, n_slots):
    L, D = x.shape
    tb = min(DISPATCH_TOKENS, L)
    grid_spec = pltpu.PrefetchScalarGridSpec(
        num_scalar_prefetch=3,
        grid=(L // tb,),
        in_specs=[pl.BlockSpec(memory_space=pl.ANY)],
        out_specs=pl.BlockSpec(memory_space=pl.ANY),
        scratch_shapes=[pltpu.VMEM((SUBLANES, D), x.dtype), pltpu.SemaphoreType.DMA(())],
    )
    return pl.pallas_call(
        functools.partial(_dispatch_kernel, n_tokens=L, tb=tb),
        grid_spec=grid_spec,
        out_shape=jax.ShapeDtypeStruct((n_slots, D), x.dtype),
        compiler_params=_params("arbitrary"),
        name="moe_dispatch",
    )(dest, pad_lo, pad_hi, x)


def _expert_kernel(blk_ref, bexp_ref, nvalid_ref, x_ref, wg_ref, wu_ref, wd_ref, o_ref):
    @pl.when(pl.program_id(0) < nvalid_ref[0])
    def _():
        x = x_ref[...].astype(BF16)
        hg = jnp.dot(x, wg_ref[0], preferred_element_type=F32)
        hu = jnp.dot(x, wu_ref[0], preferred_element_type=F32)
        a = (hg * jax.nn.sigmoid(hg) * hu).astype(BF16)
        o_ref[...] = jnp.dot(a, wd_ref[0], preferred_element_type=F32).astype(o_ref.dtype)


def _expert_ffn(xs, w_gate, w_up, w_down, blk_idx, blk_expert, n_valid):
    P, D = xs.shape
    F = w_gate.shape[-1]
    nb = P // DISPATCH_BLOCK
    grid_spec = pltpu.PrefetchScalarGridSpec(
        num_scalar_prefetch=3,
        grid=(nb,),
        in_specs=[pl.BlockSpec((DISPATCH_BLOCK, D), lambda b, blk, be, nv: (blk[b], 0)),
                  pl.BlockSpec((1, D, F), lambda b, blk, be, nv: (be[b], 0, 0)),
                  pl.BlockSpec((1, D, F), lambda b, blk, be, nv: (be[b], 0, 0)),
                  pl.BlockSpec((1, F, D), lambda b, blk, be, nv: (be[b], 0, 0))],
        out_specs=pl.BlockSpec((DISPATCH_BLOCK, D), lambda b, blk, be, nv: (blk[b], 0)),
    )
    return pl.pallas_call(
        _expert_kernel,
        grid_spec=grid_spec,
        out_shape=jax.ShapeDtypeStruct((P, D), xs.dtype),
        compiler_params=_params("arbitrary"),
        name="moe_expert_ffn",
    )(blk_idx, blk_expert, n_valid, xs, w_gate, w_up, w_down)


COMBINE_TOKENS = 256


def _combine_kernel(dest_ref, ys_hbm, x_ref, gate_ref, g_ref, b_ref, of_ref, ob_ref, buf, sem,
                    *, n_tokens, tb, alpha):
    base = pl.program_id(0) * tb

    def row_copy(k, r, slot):
        return pltpu.make_async_copy(ys_hbm.at[pl.ds(slot, 1)], buf.at[k, pl.ds(r, 1)], sem)

    def issue(j, carry):
        for u in range(DMA_UNROLL):
            r = j * DMA_UNROLL + u
            for k in range(TOP_K):
                row_copy(k, r, dest_ref[k * n_tokens + base + r]).start()
        return carry

    lax.fori_loop(0, tb // DMA_UNROLL, issue, 0)

    def drain(j, carry):
        for u in range(DMA_UNROLL * TOP_K):
            row_copy(0, 0, 0).wait()
        return carry

    lax.fori_loop(0, tb // DMA_UNROLL, drain, 0)

    g = g_ref[...]
    b = b_ref[...]

    def body(r, carry):
        rows = pl.ds(pl.multiple_of(r * LN_ROWS, LN_ROWS), LN_ROWS)
        gates = gate_ref[rows, :]
        m = buf[0, rows, :] * gates[:, 0:1]
        for k in range(1, TOP_K):
            m = m + buf[k, rows, :] * gates[:, k:k + 1]
        out = _layer_norm_rows(alpha * x_ref[rows, :] + m, g, b)
        of_ref[rows, :] = out
        ob_ref[rows, :] = out.astype(BF16)
        return carry

    lax.fori_loop(0, tb // LN_ROWS, body, 0)


def _combine(ys, x, dest, gate_cols, g, b, alpha):
    L, D = x.shape
    tb = min(COMBINE_TOKENS, L)
    grid_spec = pltpu.PrefetchScalarGridSpec(
        num_scalar_prefetch=1,
        grid=(L // tb,),
        in_specs=[pl.BlockSpec(memory_space=pl.ANY),
                  pl.BlockSpec((tb, D), lambda i, d: (i, 0)),
                  pl.BlockSpec((tb, TOP_K), lambda i, d: (i, 0)),
                  pl.BlockSpec((1, D), lambda i, d: (0, 0)),
                  pl.BlockSpec((1, D), lambda i, d: (0, 0))],
        out_specs=[pl.BlockSpec((tb, D), lambda i, d: (i, 0)),
                   pl.BlockSpec((tb, D), lambda i, d: (i, 0))],
        scratch_shapes=[pltpu.VMEM((TOP_K, tb, D), ys.dtype), pltpu.SemaphoreType.DMA(())],
    )
    return pl.pallas_call(
        functools.partial(_combine_kernel, n_tokens=L, tb=tb, alpha=alpha),
        grid_spec=grid_spec,
        out_shape=[jax.ShapeDtypeStruct((L, D), F32), jax.ShapeDtypeStruct((L, D), BF16)],
        compiler_params=_params("arbitrary"),
        name="moe_combine_layernorm",
    )(dest, ys, x, gate_cols, g.reshape(1, D), b.reshape(1, D))


def _moe_layer(x, router_w, router_b, w_gate, w_up, w_down, g, b, alpha):
    L, D = x.shape
    blk = DISPATCH_BLOCK
    expert, gate = _route(x, router_w, router_b)
    rank, counts = _expert_ranks(expert)
    padded = (counts + blk - 1) // blk * blk
    pends = jnp.cumsum(padded)
    pstarts = pends - padded
    nb = (L * TOP_K) // blk + N_EXPERTS
    n_valid = (pends[-1] // blk).astype(jnp.int32)
    dest = (jnp.sum(jnp.where(expert[None] == jnp.arange(N_EXPERTS)[:, None, None],
                              pstarts[:, None, None], 0), axis=0) + rank).astype(jnp.int32)
    blk_idx = jnp.minimum(jnp.arange(nb, dtype=jnp.int32), n_valid - 1)
    blk_expert = jnp.minimum(
        jnp.searchsorted(pends, blk_idx * blk, side="right"), N_EXPERTS - 1).astype(jnp.int32)
    dest_flat = dest.reshape(TOP_K * L)
    xs = _dispatch(x, dest_flat, (pstarts + counts).astype(jnp.int32), pends.astype(jnp.int32), nb * blk)
    ys = _expert_ffn(xs, w_gate, w_up, w_down, blk_idx, blk_expert, n_valid.reshape(1))
    return _combine(ys, x, dest_flat, jnp.transpose(gate), g, b, alpha)


def kernel(x, gla_w_in, gla_w_gate2, gla_b_gate, gla_norm_g, gla_w_out, sgu_w_in, sgu_ln_g, sgu_ln_b,
           sgu_w_s, sgu_b_s, sgu_w_out, router_w, router_b, moe_w_gate, moe_w_up, moe_w_down, ln_g, ln_b):
    B, L, D = x.shape
    depth = ln_g.shape[0]
    alpha = float((2 * depth) ** 0.25)
    outs = []
    for bi in range(B):
        xf = x[bi]
        xb = xf.astype(BF16)
        for i in range(depth):
            j = i // 2
            if i % 2 == 0:
                a = _gla_mixer(xb, gla_w_in[j], gla_w_gate2[j], gla_b_gate[j], gla_norm_g[j])
                w_out = gla_w_out[j]
            else:
                zact = _matmul(xb, sgu_w_in[j].astype(BF16), BF16, act="gelu")
                a = _sgu_gate(zact, sgu_ln_g[j], sgu_ln_b[j], sgu_w_s[j], sgu_b_s[j])
                w_out = sgu_w_out[j]
            xf, xb = _matmul_res_ln(a, w_out.astype(BF16), xf, ln_g[i, 0], ln_b[i, 0], alpha)
            xf, xb = _moe_layer(xf, router_w, router_b, moe_w_gate[i].astype(BF16),
                                moe_w_up[i].astype(BF16), moe_w_down[i].astype(BF16),
                                ln_g[i, 1], ln_b[i, 1], alpha)
        outs.append(xf)
    return jnp.stack(outs, axis=0)
```

```python
import functools

import jax
import jax.numpy as jnp
from jax import lax
from jax.experimental import pallas as pl
from jax.experimental.pallas import tpu as pltpu

GLA_HEADS = 8
GLA_GATE_RANK = 16
GLA_TAU = 16.0
GLA_CHUNK = 64
SGU_GROUPS = 8
SGU_CHUNK = 128
N_EXPERTS = 16
N_GROUPS = 4
EXPERTS_PER_GROUP = N_EXPERTS // N_GROUPS
TOP_K = 2
LN_EPS = 1e-5
RMS_EPS = 1e-6

LANES = 128
SUBLANES = 8
VMEM_LIMIT_BYTES = 56 * 1024 * 1024

DISPATCH_BLOCK = 256
NEG_BIG = -1e30

F32 = jnp.float32
BF16 = jnp.bfloat16
_NT = (((1,), (1,)), ((), ()))
_TN = (((0,), (0,)), ((), ()))


def _params(*sem):
    return pltpu.CompilerParams(dimension_semantics=sem, vmem_limit_bytes=VMEM_LIMIT_BYTES)


def _layer_norm_rows(y, g, b):
    mu = jnp.mean(y, axis=-1, keepdims=True)
    yc = y - mu
    var = jnp.mean(yc * yc, axis=-1, keepdims=True)
    return yc * lax.rsqrt(var + LN_EPS) * g + b


def _mm_kernel(x_ref, w_ref, o_ref, *, act):
    acc = jnp.dot(x_ref[...], w_ref[...], preferred_element_type=F32)
    if act == "gelu":
        acc = 0.5 * acc * (1.0 + lax.erf(acc * (2.0 ** -0.5)))
    o_ref[...] = acc.astype(o_ref.dtype)


def _tile(dim, target, align):
    t = min(target, dim) // align * align
    while dim % t:
        t -= align
    return t


def _matmul(x, w, out_dtype, act=None, tm=1024, tn=1024):
    m, k = x.shape
    n = w.shape[1]
    tm, tn = _tile(m, tm, SUBLANES), _tile(n, tn, LANES)
    return pl.pallas_call(
        functools.partial(_mm_kernel, act=act),
        grid=(m // tm, n // tn),
        in_specs=[pl.BlockSpec((tm, k), lambda i, j: (i, 0)),
                  pl.BlockSpec((k, tn), lambda i, j: (0, j))],
        out_specs=pl.BlockSpec((tm, tn), lambda i, j: (i, j)),
        out_shape=jax.ShapeDtypeStruct((m, n), out_dtype),
        compiler_params=_params("parallel", "arbitrary"),
        name="dense_matmul",
    )(x, w)


LN_ROWS = 32


def _mm_ln_kernel(a_ref, w_ref, res_ref, g_ref, b_ref, of_ref, ob_ref, *, alpha, nj, tm, tn):
    j = pl.program_id(1)
    y = alpha * res_ref[...] + jnp.dot(a_ref[...], w_ref[...], preferred_element_type=F32)
    for jj in range(nj):
        @pl.when(j == jj)
        def _():
            of_ref[:, jj * tn:(jj + 1) * tn] = y

    @pl.when(j == nj - 1)
    def _():
        g = g_ref[...]
        b = b_ref[...]

        def body(r, carry):
            rows = pl.ds(pl.multiple_of(r * LN_ROWS, LN_ROWS), LN_ROWS)
            out = _layer_norm_rows(of_ref[rows, :], g, b)
            of_ref[rows, :] = out
            ob_ref[rows, :] = out.astype(BF16)
            return carry

        lax.fori_loop(0, tm // LN_ROWS, body, 0)


def _matmul_res_ln(a, w, res, g, b, alpha, tm=512, tn=512):
    m, kdim = a.shape
    n = w.shape[1]
    tm, tn = _tile(m, tm, SUBLANES), _tile(n, tn, LANES)
    nj = n // tn
    return pl.pallas_call(
        functools.partial(_mm_ln_kernel, alpha=alpha, nj=nj, tm=tm, tn=tn),
        grid=(m // tm, nj),
        in_specs=[pl.BlockSpec((tm, kdim), lambda i, j: (i, 0)),
                  pl.BlockSpec((kdim, tn), lambda i, j: (0, j)),
                  pl.BlockSpec((tm, tn), lambda i, j: (i, j)),
                  pl.BlockSpec((1, n), lambda i, j: (0, 0)),
                  pl.BlockSpec((1, n), lambda i, j: (0, 0))],
        out_specs=[pl.BlockSpec((tm, n), lambda i, j: (i, 0)),
                   pl.BlockSpec((tm, n), lambda i, j: (i, 0))],
        out_shape=[jax.ShapeDtypeStruct((m, n), F32),
                   jax.ShapeDtypeStruct((m, n), BF16)],
        compiler_params=_params("parallel", "arbitrary"),
        name="matmul_residual_layernorm",
    )(a, w, res, g.reshape(1, n), b.reshape(1, n))


def _log_sigmoid(x):
    return jnp.minimum(x, 0.0) - jnp.log1p(jnp.exp(-jnp.abs(x)))


def _gla_kernel(*refs, reverse, finalize, n_chunks, scale):
    if finalize:
        (q_ref, k_ref, v_ref, z_ref, wg_ref, bg_ref, oprev_ref, r_ref, ng_ref, o_ref,
         g_scr, st_scr, qd_scr, kd_scr, ks_scr, oi_scr, u_scr) = refs
    else:
        (q_ref, k_ref, v_ref, z_ref, wg_ref, bg_ref, o_ref,
         g_scr, st_scr, qd_scr, kd_scr, ks_scr, oi_scr, u_scr) = refs
    C = GLA_CHUNK

    @pl.when(pl.program_id(1) == 0)
    def _():
        st_scr[...] = jnp.zeros_like(st_scr)

    zh, zl = _split_bf16(z_ref[...])
    wh, wl = _split_bf16(wg_ref[...])
    logit = (jnp.dot(zh, wh, preferred_element_type=F32) + jnp.dot(zh, wl, preferred_element_type=F32)
             + jnp.dot(zl, wh, preferred_element_type=F32)) + bg_ref[...]
    g_scr[...] = _log_sigmoid(logit) * (1.0 / GLA_TAU)

    row = lax.broadcasted_iota(jnp.int32, (C, C), 0)
    col = lax.broadcasted_iota(jnp.int32, (C, C), 1)
    if reverse:
        tri = jnp.where(col >= row, 1.0, 0.0).astype(BF16)
        mask = col > row
        last = 0
    else:
        tri = jnp.where(col <= row, 1.0, 0.0).astype(BF16)
        mask = col <= row
        last = C - 1

    chunk_rows = [slice(c * C, (c + 1) * C) for c in range(n_chunks)]
    decays = []
    for rows in chunk_rows:
        g = g_scr[rows, :]
        g_hi = g.astype(BF16)
        g_mid = (g - g_hi.astype(F32)).astype(BF16)
        g_lo = (g - g_hi.astype(F32) - g_mid.astype(F32)).astype(BF16)
        b = (jnp.dot(tri, g_hi, preferred_element_type=F32)
             + jnp.dot(tri, g_mid, preferred_element_type=F32)
             + jnp.dot(tri, g_lo, preferred_element_type=F32))
        b_last = b[last:last + 1, :]
        q = q_ref[rows, :].astype(F32) * scale
        k = k_ref[rows, :].astype(F32)
        qd_scr[rows, :] = (q * jnp.exp(b)).astype(BF16)
        kd_scr[rows, :] = (k * jnp.exp(-b)).astype(BF16)
        ks_scr[rows, :] = (k * jnp.exp(b_last - b)).astype(BF16)
        decays.append(jnp.exp(b_last))
    scores = [lax.dot_general(qd_scr[rows, :], kd_scr[rows, :], _NT, preferred_element_type=F32)
              for rows in chunk_rows]
    for c, rows in enumerate(chunk_rows):
        u_scr[c] = lax.dot_general(v_ref[rows, :], ks_scr[rows, :], _TN, preferred_element_type=F32)
    for c, rows in enumerate(chunk_rows):
        masked = jnp.where(mask, scores[c], 0.0).astype(BF16)
        oi_scr[rows, :] = jnp.dot(masked, v_ref[rows, :], preferred_element_type=F32)
    st = st_scr[...]
    for c in (reversed(range(n_chunks)) if reverse else range(n_chunks)):
        rows = chunk_rows[c]
        o = oi_scr[rows, :] + lax.dot_general(qd_scr[rows, :], st.astype(BF16), _NT,
                                              preferred_element_type=F32)
        st = decays[c] * st + u_scr[c]
        if finalize:
            o = o + oprev_ref[rows, :].astype(F32)
            o = o * lax.rsqrt(jnp.mean(o * o, axis=-1, keepdims=True) + RMS_EPS)
            r = r_ref[rows, :].astype(F32)
            o = o * ng_ref[...] * (r * jax.nn.sigmoid(r))
        o_ref[rows, :] = o.astype(o_ref.dtype)
    st_scr[...] = st


def _gla_direction(proj, z, wg_pad, b_gate, *, reverse, o_prev=None, norm_g=None, tl=512):
    L = proj.shape[0]
    H = GLA_HEADS
    dk_total = wg_pad.shape[1]
    dk = dk_total // H
    dv_total = (proj.shape[1] - 2 * dk_total) // 2
    dv = dv_total // H
    tl = min(tl, L)
    nb = L // tl
    finalize = o_prev is not None
    blk = (lambda i: nb - 1 - i) if reverse else (lambda i: i)
    k_off = dk_total // dk
    v_off = 2 * dk_total // dv
    r_off = v_off + dv_total // dv

    in_specs = [
        pl.BlockSpec((tl, dk), lambda h, i: (blk(i), h)),
        pl.BlockSpec((tl, dk), lambda h, i: (blk(i), k_off + h)),
        pl.BlockSpec((tl, dv), lambda h, i: (blk(i), v_off + h)),
        pl.BlockSpec((tl, LANES), lambda h, i: (blk(i), 0)),
        pl.BlockSpec((LANES, dk), lambda h, i: (0, h)),
        pl.BlockSpec((1, dk), lambda h, i: (0, h)),
    ]
    args = [proj, proj, proj, z, wg_pad, b_gate.reshape(1, dk_total)]
    if finalize:
        in_specs += [
            pl.BlockSpec((tl, dv), lambda h, i: (blk(i), h)),
            pl.BlockSpec((tl, dv), lambda h, i: (blk(i), r_off + h)),
            pl.BlockSpec((1, dv), lambda h, i: (0, h)),
        ]
        args += [o_prev, proj, norm_g.reshape(1, dv_total)]
    return pl.pallas_call(
        functools.partial(_gla_kernel, reverse=reverse, finalize=finalize,
                          n_chunks=tl // GLA_CHUNK, scale=float(dk) ** -0.5),
        grid=(H, nb),
        in_specs=in_specs,
        out_specs=pl.BlockSpec((tl, dv), lambda h, i: (blk(i), h)),
        out_shape=jax.ShapeDtypeStruct((L, dv_total), BF16),
        scratch_shapes=[pltpu.VMEM((tl, dk), F32), pltpu.VMEM((dv, dk), F32),
                        pltpu.VMEM((tl, dk), BF16), pltpu.VMEM((tl, dk), BF16), pltpu.VMEM((tl, dk), BF16),
                        pltpu.VMEM((tl, dv), F32), pltpu.VMEM((tl // GLA_CHUNK, dv, dk), F32)],
        compiler_params=_params("parallel", "arbitrary"),
        name="gla_backward_scan" if reverse else "gla_forward_scan",
    )(*args)


def _gla_mixer(xb, w_in, w_gate2, b_gate, norm_g):
    dk_total = w_gate2.shape[-1]
    n_main = w_in.shape[1] - 2 * GLA_GATE_RANK
    rank = GLA_GATE_RANK
    proj = _matmul(xb, w_in[:, :n_main].astype(BF16), BF16)
    w_z = jnp.zeros((w_in.shape[0], LANES), BF16).at[:, :2 * rank].set(w_in[:, n_main:].astype(BF16))
    z = _matmul(xb, w_z, F32)
    wg_f = jnp.zeros((LANES, dk_total), F32).at[:rank].set(w_gate2[0])
    wg_b = jnp.zeros((LANES, dk_total), F32).at[rank:2 * rank].set(w_gate2[1])
    o_f = _gla_direction(proj, z, wg_f, b_gate[0], reverse=False)
    return _gla_direction(proj, z, wg_b, b_gate[1], reverse=True, o_prev=o_f, norm_g=norm_g)


def _sgu_kernel(u_ref, v_ref, g_ref, b_ref, ws_ref, bias_ref, o_ref, *, n_chunks, gw):
    C = SGU_CHUNK
    for c in range(n_chunks):
        rows = slice(c * C, (c + 1) * C)
        vn = _layer_norm_rows(v_ref[rows, :].astype(F32), g_ref[...], b_ref[...]).astype(BF16)
        for grp in range(SGU_GROUPS):
            cols = slice(grp * gw, (grp + 1) * gw)
            mixed = jnp.dot(ws_ref[grp], vn[:, cols], preferred_element_type=F32) + bias_ref[:, cols]
            o_ref[rows, cols] = (u_ref[rows, cols].astype(F32) * mixed).astype(BF16)


def _sgu_gate(zact, ln_g, ln_b, w_s, b_s, tl=256):
    L = zact.shape[0]
    half = zact.shape[1] // 2
    gw = half // SGU_GROUPS
    tl = min(tl, L)
    bias = jnp.repeat(jnp.transpose(b_s), gw, axis=1)
    return pl.pallas_call(
        functools.partial(_sgu_kernel, n_chunks=tl // SGU_CHUNK, gw=gw),
        grid=(L // tl,),
        in_specs=[pl.BlockSpec((tl, half), lambda i: (i, 0)),
                  pl.BlockSpec((tl, half), lambda i: (i, 1)),
                  pl.BlockSpec((1, half), lambda i: (0, 0)),
                  pl.BlockSpec((1, half), lambda i: (0, 0)),
                  pl.BlockSpec((SGU_GROUPS, SGU_CHUNK, SGU_CHUNK), lambda i: (0, 0, 0)),
                  pl.BlockSpec((SGU_CHUNK, half), lambda i: (0, 0))],
        out_specs=pl.BlockSpec((tl, half), lambda i: (i, 0)),
        out_shape=jax.ShapeDtypeStruct((L, half), BF16),
        compiler_params=_params("parallel"),
        name="sgu_spatial_gate",
    )(zact, zact, ln_g.reshape(1, half), ln_b.reshape(1, half), w_s.astype(BF16), bias)


ROUTER_ROWS = EXPERTS_PER_GROUP * SUBLANES


def _split_bf16(a):
    hi = a.astype(BF16)
    lo = (a - hi.astype(F32)).astype(BF16)
    return hi, lo


def _router_kernel(x_ref, wt_ref, rb_ref, o_ref):
    xh, xl = _split_bf16(x_ref[...])
    wh, wl = _split_bf16(wt_ref[...])
    logits = (lax.dot_general(wh, xh, _NT, preferred_element_type=F32)
              + lax.dot_general(wh, xl, _NT, preferred_element_type=F32)
              + lax.dot_general(wl, xh, _NT, preferred_element_type=F32))
    scores = jax.nn.sigmoid(logits)
    sel = scores + rb_ref[:, 0:1]
    T = logits.shape[1]
    s_m = [scores[SUBLANES * m:SUBLANES * (m + 1), :] for m in range(EXPERTS_PER_GROUP)]
    a_m = [sel[SUBLANES * m:SUBLANES * (m + 1), :] for m in range(EXPERTS_PER_GROUP)]
    hi1, lo1 = jnp.maximum(a_m[0], a_m[1]), jnp.minimum(a_m[0], a_m[1])
    hi2, lo2 = jnp.maximum(a_m[2], a_m[3]), jnp.minimum(a_m[2], a_m[3])
    group_score = jnp.maximum(hi1, hi2) + jnp.maximum(jnp.minimum(hi1, hi2), jnp.maximum(lo1, lo2))
    gid = lax.broadcasted_iota(jnp.int32, (SUBLANES, T), 0)
    gmax = jnp.max(group_score, axis=0, keepdims=True)
    g_star = jnp.min(jnp.where(group_score == gmax, gid, SUBLANES), axis=0, keepdims=True)
    onehot = gid == g_star
    v_m = [jnp.sum(jnp.where(onehot, a, 0.0), axis=0, keepdims=True) for a in a_m]
    r_m = [jnp.sum(jnp.where(onehot, s, 0.0), axis=0, keepdims=True) for s in s_m]
    best1, idx1, raw1 = v_m[0], jnp.zeros_like(g_star), r_m[0]
    for m in range(1, EXPERTS_PER_GROUP):
        take = v_m[m] > best1
        best1 = jnp.where(take, v_m[m], best1)
        idx1 = jnp.where(take, m, idx1)
        raw1 = jnp.where(take, r_m[m], raw1)
    best2 = jnp.full_like(best1, -jnp.inf)
    idx2, raw2 = jnp.zeros_like(g_star), jnp.zeros_like(raw1)
    for m in range(EXPERTS_PER_GROUP):
        take = jnp.logical_and(idx1 != m, v_m[m] > best2)
        best2 = jnp.where(take, v_m[m], best2)
        idx2 = jnp.where(take, m, idx2)
        raw2 = jnp.where(take, r_m[m], raw2)
    denom = raw1 + raw2
    e1 = (g_star * EXPERTS_PER_GROUP + idx1).astype(F32)
    e2 = (g_star * EXPERTS_PER_GROUP + idx2).astype(F32)
    rid = lax.broadcasted_iota(jnp.int32, (SUBLANES, T), 0)
    out = jnp.where(rid == 0, e1, jnp.where(rid == 1, e2, jnp.where(
        rid == 2, raw1 / denom, jnp.where(rid == 3, raw2 / denom, 0.0))))
    o_ref[...] = out


def _route(x, router_w, router_b, tl=512):
    L, D = x.shape
    tl = min(tl, L)
    w_t = jnp.zeros((EXPERTS_PER_GROUP, SUBLANES, D), F32)
    w_t = w_t.at[:, :N_GROUPS].set(router_w.T.reshape(N_GROUPS, EXPERTS_PER_GROUP, D).transpose(1, 0, 2))
    rb = jnp.full((EXPERTS_PER_GROUP, SUBLANES), NEG_BIG, F32)
    rb = rb.at[:, :N_GROUPS].set(router_b.astype(F32).reshape(N_GROUPS, EXPERTS_PER_GROUP).T)
    rb = jnp.broadcast_to(rb.reshape(ROUTER_ROWS, 1), (ROUTER_ROWS, LANES))
    out = pl.pallas_call(
        _router_kernel,
        grid=(L // tl,),
        in_specs=[pl.BlockSpec((tl, D), lambda i: (i, 0)),
                  pl.BlockSpec((ROUTER_ROWS, D), lambda i: (0, 0)),
                  pl.BlockSpec((ROUTER_ROWS, LANES), lambda i: (0, 0))],
        out_specs=pl.BlockSpec((SUBLANES, tl), lambda i: (0, i)),
        out_shape=jax.ShapeDtypeStruct((SUBLANES, L), F32),
        compiler_params=_params("parallel"),
        name="moe_router",
    )(x, w_t.reshape(ROUTER_ROWS, D), rb)
    expert = out[:TOP_K].astype(jnp.int32)
    gate = out[TOP_K:2 * TOP_K]
    return expert, gate


def _rank_kernel(e_ref, rank_ref, cnt_ref, carry_scr):
    @pl.when(jnp.logical_and(pl.program_id(0) == 0, pl.program_id(1) == 0))
    def _():
        carry_scr[...] = jnp.zeros_like(carry_scr)

    e = e_ref[0]
    T = e.shape[1]
    eid = lax.broadcasted_iota(jnp.int32, (N_EXPERTS, T), 0)
    onehot = eid == e
    oh = jnp.where(onehot, 1.0, 0.0)
    s_idx = lax.broadcasted_iota(jnp.int32, (T, T), 0)
    t_idx = lax.broadcasted_iota(jnp.int32, (T, T), 1)
    before = jnp.where(s_idx < t_idx, 1.0, 0.0).astype(BF16)
    prefix = jnp.dot(oh.astype(BF16), before, preferred_element_type=F32)
    carry = carry_scr[:, 0:1]
    rank_ref[0] = jnp.sum(jnp.where(onehot, prefix + carry, 0.0), axis=0, keepdims=True)
    carry_scr[...] = carry_scr[...] + jnp.sum(oh, axis=1, keepdims=True)
    cnt_ref[...] = carry_scr[...]


def _expert_ranks(expert, tl=512):
    K, L = expert.shape
    tl = min(tl, L)
    rank, cnt = pl.pallas_call(
        _rank_kernel,
        grid=(K, L // tl),
        in_specs=[pl.BlockSpec((1, 1, tl), lambda k, i: (k, 0, i))],
        out_specs=[pl.BlockSpec((1, 1, tl), lambda k, i: (k, 0, i)),
                   pl.BlockSpec((N_EXPERTS, LANES), lambda k, i: (0, 0))],
        out_shape=[jax.ShapeDtypeStruct((K, 1, L), F32),
                   jax.ShapeDtypeStruct((N_EXPERTS, LANES), F32)],
        scratch_shapes=[pltpu.VMEM((N_EXPERTS, LANES), F32)],
        compiler_params=_params("arbitrary", "arbitrary"),
        name="moe_expert_rank",
    )(expert.reshape(K, 1, L))
    return rank.reshape(K, L).astype(jnp.int32), cnt[:, 0].astype(jnp.int32)


DMA_UNROLL = 8


def _slot_token_kernel(dest_ref, tok_ref, *, n_tokens, n_slots):
    def clear(j, carry):
        for u in range(DMA_UNROLL):
            tok_ref[j * DMA_UNROLL + u] = 0
        return carry

    lax.fori_loop(0, n_slots // DMA_UNROLL, clear, 0)
    for k in range(TOP_K):
        def scatter(j, carry):
            for u in range(DMA_UNROLL):
                t = j * DMA_UNROLL + u
                tok_ref[dest_ref[k * n_tokens + t]] = t
            return carry

        lax.fori_loop(0, n_tokens // DMA_UNROLL, scatter, 0)


def _slot_tokens(dest, n_tokens, n_slots):
    return pl.pallas_call(
        functools.partial(_slot_token_kernel, n_tokens=n_tokens, n_slots=n_slots),
        in_specs=[pl.BlockSpec(memory_space=pltpu.SMEM)],
        out_specs=pl.BlockSpec(memory_space=pltpu.SMEM),
        out_shape=jax.ShapeDtypeStruct((n_slots,), jnp.int32),
        name="moe_slot_tokens",
    )(dest)


def _expert_kernel(tok_ref, bexp_ref, nvalid_ref, x_hbm, wg_ref, wu_ref, wd_ref, o_ref, xbuf, sem):
    b = pl.program_id(0)
    n_valid = nvalid_ref[0]
    rows = xbuf.shape[1]

    def row_copy(slot, r, tok):
        return pltpu.make_async_copy(x_hbm.at[pl.ds(tok, 1)], xbuf.at[slot, pl.ds(r, 1)], sem.at[slot])

    def gather(blk, slot):
        base = blk * rows

        def issue(j, carry):
            for u in range(DMA_UNROLL):
                r = j * DMA_UNROLL + u
                row_copy(slot, r, tok_ref[base + r]).start()
            return carry

        lax.fori_loop(0, rows // DMA_UNROLL, issue, 0)

    @pl.when(b == 0)
    def _():
        gather(0, 0)

    @pl.when(b + 1 < n_valid)
    def _():
        gather(b + 1, (b + 1) % 2)

    @pl.when(b < n_valid)
    def _():
        slot = b % 2

        def drain(j, carry):
            for u in range(DMA_UNROLL):
                row_copy(slot, 0, 0).wait()
            return carry

        lax.fori_loop(0, rows // DMA_UNROLL, drain, 0)
        x = xbuf[slot].astype(BF16)
        hg = jnp.dot(x, wg_ref[0], preferred_element_type=F32)
        hu = jnp.dot(x, wu_ref[0], preferred_element_type=F32)
        a = (hg * jax.nn.sigmoid(hg) * hu).astype(BF16)
        o_ref[...] = jnp.dot(a, wd_ref[0], preferred_element_type=F32).astype(o_ref.dtype)

    @pl.when(b >= n_valid)
    def _():
        o_ref[...] = jnp.zeros_like(o_ref)


def _expert_ffn(x, slot_tok, w_gate, w_up, w_down, blk_expert, n_valid):
    D = x.shape[1]
    F = w_gate.shape[-1]
    P = slot_tok.shape[0]
    nb = P // DISPATCH_BLOCK
    grid_spec = pltpu.PrefetchScalarGridSpec(
        num_scalar_prefetch=3,
        grid=(nb,),
        in_specs=[pl.BlockSpec(memory_space=pl.ANY),
                  pl.BlockSpec((1, D, F), lambda b, tok, be, nv: (be[b], 0, 0)),
                  pl.BlockSpec((1, D, F), lambda b, tok, be, nv: (be[b], 0, 0)),
                  pl.BlockSpec((1, F, D), lambda b, tok, be, nv: (be[b], 0, 0))],
        out_specs=pl.BlockSpec((DISPATCH_BLOCK, D), lambda b, tok, be, nv: (b, 0)),
        scratch_shapes=[pltpu.VMEM((2, DISPATCH_BLOCK, D), x.dtype), pltpu.SemaphoreType.DMA((2,))],
    )
    return pl.pallas_call(
        _expert_kernel,
        grid_spec=grid_spec,
        out_shape=jax.ShapeDtypeStruct((P, D), F32),
        compiler_params=_params("arbitrary"),
        name="moe_expert_ffn",
    )(slot_tok, blk_expert, n_valid, x, w_gate, w_up, w_down)


COMBINE_TOKENS = 256


def _combine_kernel(dest_ref, ys_hbm, x_ref, gate_ref, g_ref, b_ref, of_ref, ob_ref, buf, sem,
                    *, n_tokens, tb, alpha):
    i = pl.program_id(0)

    def row_copy(half, k, r, src):
        return pltpu.make_async_copy(ys_hbm.at[pl.ds(src, 1)], buf.at[half, k, pl.ds(r, 1)], sem.at[half])

    def gather(step, half):
        base = step * tb

        def issue(j, carry):
            for u in range(DMA_UNROLL):
                r = j * DMA_UNROLL + u
                for k in range(TOP_K):
                    row_copy(half, k, r, dest_ref[k * n_tokens + base + r]).start()
            return carry

        lax.fori_loop(0, tb // DMA_UNROLL, issue, 0)

    @pl.when(i == 0)
    def _():
        gather(0, 0)

    @pl.when(i + 1 < pl.num_programs(0))
    def _():
        gather(i + 1, (i + 1) % 2)

    half = i % 2

    def drain(j, carry):
        for u in range(DMA_UNROLL * TOP_K):
            row_copy(half, 0, 0, 0).wait()
        return carry

    lax.fori_loop(0, tb // DMA_UNROLL, drain, 0)

    g = g_ref[...]
    b = b_ref[...]

    def body(r, carry):
        rows = pl.ds(pl.multiple_of(r * LN_ROWS, LN_ROWS), LN_ROWS)
        gates = gate_ref[rows, :]
        m = buf[half, 0, rows, :] * gates[:, 0:1]
        for k in range(1, TOP_K):
            m = m + buf[half, k, rows, :] * gates[:, k:k + 1]
        out = _layer_norm_rows(alpha * x_ref[rows, :] + m, g, b)
        of_ref[rows, :] = out
        ob_ref[rows, :] = out.astype(BF16)
        return carry

    lax.fori_loop(0, tb // LN_ROWS, body, 0)


def _combine(ys, x, dest, gate_cols, g, b, alpha):
    L, D = x.shape
    tb = min(COMBINE_TOKENS, L)
    grid_spec = pltpu.PrefetchScalarGridSpec(
        num_scalar_prefetch=1,
        grid=(L // tb,),
        in_specs=[pl.BlockSpec(memory_space=pl.ANY),
                  pl.BlockSpec((tb, D), lambda i, d: (i, 0)),
                  pl.BlockSpec((tb, TOP_K), lambda i, d: (i, 0)),
                  pl.BlockSpec((1, D), lambda i, d: (0, 0)),
                  pl.BlockSpec((1, D), lambda i, d: (0, 0))],
        out_specs=[pl.BlockSpec((tb, D), lambda i, d: (i, 0)),
                   pl.BlockSpec((tb, D), lambda i, d: (i, 0))],
        scratch_shapes=[pltpu.VMEM((2, TOP_K, tb, D), ys.dtype), pltpu.SemaphoreType.DMA((2,))],
    )
    return pl.pallas_call(
        functools.partial(_combine_kernel, n_tokens=L, tb=tb, alpha=alpha),
        grid_spec=grid_spec,
        out_shape=[jax.ShapeDtypeStruct((L, D), F32), jax.ShapeDtypeStruct((L, D), BF16)],
        compiler_params=_params("arbitrary"),
        name="moe_combine_layernorm",
    )(dest, ys, x, gate_cols, g.reshape(1, D), b.reshape(1, D))


def _moe_layer(x, router_w, router_b, w_gate, w_up, w_down, g, b, alpha):
    L, D = x.shape
    blk = DISPATCH_BLOCK
    expert, gate = _route(x, router_w, router_b)
    rank, counts = _expert_ranks(expert)
    padded = (counts + blk - 1) // blk * blk
    pends = jnp.cumsum(padded)
    pstarts = pends - padded
    nb = (L * TOP_K) // blk + N_EXPERTS
    n_valid = (pends[-1] // blk).astype(jnp.int32)
    dest = (jnp.sum(jnp.where(expert[None] == jnp.arange(N_EXPERTS)[:, None, None],
                              pstarts[:, None, None], 0), axis=0) + rank).astype(jnp.int32)
    blk_idx = jnp.minimum(jnp.arange(nb, dtype=jnp.int32), n_valid - 1)
    blk_expert = jnp.minimum(jnp.sum(pends[None, :] <= (blk_idx * blk)[:, None], axis=1),
                             N_EXPERTS - 1).astype(jnp.int32)
    dest_flat = dest.reshape(TOP_K * L)
    slot_tok = _slot_tokens(dest_flat, L, nb * blk)
    ys = _expert_ffn(x, slot_tok, w_gate, w_up, w_down, blk_expert, n_valid.reshape(1))
    return _combine(ys, x, dest_flat, jnp.transpose(gate), g, b, alpha)


def kernel(x, gla_w_in, gla_w_gate2, gla_b_gate, gla_norm_g, gla_w_out, sgu_w_in, sgu_ln_g, sgu_ln_b,
           sgu_w_s, sgu_b_s, sgu_w_out, router_w, router_b, moe_w_gate, moe_w_up, moe_w_down, ln_g, ln_b):
    B, L, D = x.shape
    depth = ln_g.shape[0]
    alpha = float((2 * depth) ** 0.25)
    outs = []
    for bi in range(B):
        xf = x[bi]
        xb = xf.astype(BF16)
        for i in range(depth):
            j = i // 2
            if i % 2 == 0:
                a = _gla_mixer(xb, gla_w_in[j], gla_w_gate2[j], gla_b_gate[j], gla_norm_g[j])
                w_out = gla_w_out[j]
            else:
                zact = _matmul(xb, sgu_w_in[j].astype(BF16), BF16, act="gelu")
                a = _sgu_gate(zact, sgu_ln_g[j], sgu_ln_b[j], sgu_w_s[j], sgu_b_s[j])
                w_out = sgu_w_out[j]
            xf, xb = _matmul_res_ln(a, w_out.astype(BF16), xf, ln_g[i, 0], ln_b[i, 0], alpha)
            xf, xb = _moe_layer(xf, router_w, router_b, moe_w_gate[i].astype(BF16),
                                moe_w_up[i].astype(BF16), moe_w_down[i].astype(BF16),
                                ln_g[i, 1], ln_b[i, 1], alpha)
        outs.append(xf)
    return jnp.stack(outs, axis=0)
```

```python
import functools

import jax
import jax.numpy as jnp
from jax import lax
from jax.experimental import pallas as pl
from jax.experimental.pallas import tpu as pltpu

GLA_HEADS = 8
GLA_GATE_RANK = 16
GLA_TAU = 16.0
GLA_CHUNK = 64
SGU_GROUPS = 8
SGU_CHUNK = 128
N_EXPERTS = 16
N_GROUPS = 4
EXPERTS_PER_GROUP = N_EXPERTS // N_GROUPS
TOP_K = 2
LN_EPS = 1e-5
RMS_EPS = 1e-6

LANES = 128
SUBLANES = 8
VMEM_LIMIT_BYTES = 56 * 1024 * 1024

DISPATCH_BLOCK = 256
NEG_BIG = -1e30

F32 = jnp.float32
BF16 = jnp.bfloat16
_NT = (((1,), (1,)), ((), ()))
_TN = (((0,), (0,)), ((), ()))


def _params(*sem):
    return pltpu.CompilerParams(dimension_semantics=sem, vmem_limit_bytes=VMEM_LIMIT_BYTES)


def _layer_norm_rows(y, g, b):
    mu = jnp.mean(y, axis=-1, keepdims=True)
    yc = y - mu
    var = jnp.mean(yc * yc, axis=-1, keepdims=True)
    return yc * lax.rsqrt(var + LN_EPS) * g + b


def _mm_kernel(x_ref, w_ref, o_ref, *, act):
    acc = jnp.dot(x_ref[...], w_ref[...], preferred_element_type=F32)
    if act == "gelu":
        acc = 0.5 * acc * (1.0 + lax.erf(acc * (2.0 ** -0.5)))
    o_ref[...] = acc.astype(o_ref.dtype)


def _tile(dim, target, align):
    t = min(target, dim) // align * align
    while dim % t:
        t -= align
    return t


def _matmul(x, w, layer, n, out_dtype, act=None, tm=1024, tn=1024):
    m, k = x.shape
    tm, tn = _tile(m, tm, SUBLANES), _tile(n, tn, LANES)
    return pl.pallas_call(
        functools.partial(_mm_kernel, act=act),
        grid=(m // tm, n // tn),
        in_specs=[pl.BlockSpec((tm, k), lambda i, j: (i, 0)),
                  pl.BlockSpec((None, k, tn), lambda i, j: (layer, 0, j))],
        out_specs=pl.BlockSpec((tm, tn), lambda i, j: (i, j)),
        out_shape=jax.ShapeDtypeStruct((m, n), out_dtype),
        compiler_params=_params("parallel", "arbitrary"),
        name="dense_matmul",
    )(x, w)


PACK_GROUP = 512
_HI_MASK = 0xFFFF0000


def _bf16_bits(x):
    return lax.bitcast_convert_type(x.astype(BF16).astype(F32), jnp.uint32)


def _pack_rows(x):
    half = PACK_GROUP // 2
    words = []
    for c0 in range(0, x.shape[1], PACK_GROUP):
        lo = _bf16_bits(x[:, c0:c0 + half]) >> 16
        hi = _bf16_bits(x[:, c0 + half:c0 + PACK_GROUP]) & jnp.uint32(_HI_MASK)
        words.append(hi | lo)
    return words[0] if len(words) == 1 else jnp.concatenate(words, axis=1)


def _unpack_rows(w):
    half = PACK_GROUP // 2
    cols = []
    for c0 in range(0, w.shape[1], half):
        piece = w[:, c0:c0 + half]
        cols.append(lax.bitcast_convert_type(piece << 16, F32))
        cols.append(lax.bitcast_convert_type(piece & jnp.uint32(_HI_MASK), F32))
    return jnp.concatenate(cols, axis=1)


LN_ROWS = 32


def _mm_ln_kernel(a_ref, w_ref, res_ref, g_ref, b_ref, of_ref, xp_ref, y_scr, mu_scr, rs_scr,
                  *, alpha, nj, tm, n):
    j = pl.program_id(1)

    @pl.when(j < nj)
    def _():
        y_scr[j] = alpha * res_ref[...] + jnp.dot(a_ref[...], w_ref[...], preferred_element_type=F32)

    @pl.when(j == nj)
    def _():
        def body(r, carry):
            rows = pl.ds(pl.multiple_of(r * LN_ROWS, LN_ROWS), LN_ROWS)
            total = jnp.sum(y_scr[0, rows, :], axis=-1, keepdims=True)
            for t in range(1, nj):
                total = total + jnp.sum(y_scr[t, rows, :], axis=-1, keepdims=True)
            mu = total * (1.0 / n)
            sq = jnp.zeros_like(mu)
            for t in range(nj):
                yc = y_scr[t, rows, :] - mu
                sq = sq + jnp.sum(yc * yc, axis=-1, keepdims=True)
            mu_scr[rows, :] = mu
            rs_scr[rows, :] = lax.rsqrt(sq * (1.0 / n) + LN_EPS)
            return carry

        lax.fori_loop(0, tm // LN_ROWS, body, 0)

    @pl.when(j >= nj)
    def _():
        t = j - nj
        g = g_ref[...]
        b = b_ref[...]

        def body(r, carry):
            rows = pl.ds(pl.multiple_of(r * LN_ROWS, LN_ROWS), LN_ROWS)
            out = (y_scr[t, rows, :] - mu_scr[rows, :]) * rs_scr[rows, :] * g + b
            of_ref[rows, :] = out
            xp_ref[rows, :] = _pack_rows(out)
            return carry

        lax.fori_loop(0, tm // LN_ROWS, body, 0)


def _matmul_res_ln(a, w, layer, res, g, b, alpha, tm=1024, tn=512):
    m, kdim = a.shape
    n = w.shape[2]
    tm, tn = _tile(m, tm, SUBLANES), _tile(n, tn, PACK_GROUP)
    nj = n // tn
    fill = lambda j: jnp.minimum(j, nj - 1)
    emit = lambda j: jnp.maximum(j - nj, 0)
    return pl.pallas_call(
        functools.partial(_mm_ln_kernel, alpha=alpha, nj=nj, tm=tm, n=n),
        grid=(m // tm, 2 * nj),
        in_specs=[pl.BlockSpec((tm, kdim), lambda i, j: (i, 0)),
                  pl.BlockSpec((None, kdim, tn), lambda i, j: (layer, 0, fill(j))),
                  pl.BlockSpec((tm, tn), lambda i, j: (i, fill(j))),
                  pl.BlockSpec((1, tn), lambda i, j: (0, emit(j))),
                  pl.BlockSpec((1, tn), lambda i, j: (0, emit(j)))],
        out_specs=[pl.BlockSpec((tm, tn), lambda i, j: (i, emit(j))),
                   pl.BlockSpec((tm, tn // 2), lambda i, j: (i, emit(j)))],
        out_shape=[jax.ShapeDtypeStruct((m, n), F32),
                   jax.ShapeDtypeStruct((m, n // 2), jnp.uint32)],
        scratch_shapes=[pltpu.VMEM((nj, tm, tn), F32), pltpu.VMEM((tm, 1), F32), pltpu.VMEM((tm, 1), F32)],
        compiler_params=_params("parallel", "arbitrary"),
        name="matmul_residual_layernorm",
    )(a, w, res, g.reshape(1, n), b.reshape(1, n))


def _log_sigmoid(x):
    return jnp.minimum(x, 0.0) - jnp.log1p(jnp.exp(-jnp.abs(x)))


def _gla_kernel(*refs, reverse, finalize, n_chunks, scale):
    if finalize:
        (q_ref, k_ref, v_ref, z_ref, wg_ref, bg_ref, oprev_ref, r_ref, ng_ref, o_ref,
         g_scr, st_scr, qd_scr, kd_scr, ks_scr, oi_scr, u_scr) = refs
    else:
        (q_ref, k_ref, v_ref, z_ref, wg_ref, bg_ref, o_ref,
         g_scr, st_scr, qd_scr, kd_scr, ks_scr, oi_scr, u_scr) = refs
    C = GLA_CHUNK

    @pl.when(pl.program_id(1) == 0)
    def _():
        st_scr[...] = jnp.zeros_like(st_scr)

    zh, zl = _split_bf16(z_ref[...])
    wh, wl = _split_bf16(wg_ref[...])
    logit = (jnp.dot(zh, wh, preferred_element_type=F32) + jnp.dot(zh, wl, preferred_element_type=F32)
             + jnp.dot(zl, wh, preferred_element_type=F32)) + bg_ref[...]
    g_scr[...] = _log_sigmoid(logit) * (1.0 / GLA_TAU)

    row = lax.broadcasted_iota(jnp.int32, (C, C), 0)
    col = lax.broadcasted_iota(jnp.int32, (C, C), 1)
    if reverse:
        tri = jnp.where(col >= row, 1.0, 0.0).astype(BF16)
        mask = col > row
        last = 0
    else:
        tri = jnp.where(col <= row, 1.0, 0.0).astype(BF16)
        mask = col <= row
        last = C - 1

    chunk_rows = [slice(c * C, (c + 1) * C) for c in range(n_chunks)]
    decays = []
    for rows in chunk_rows:
        g = g_scr[rows, :]
        g_hi = g.astype(BF16)
        g_mid = (g - g_hi.astype(F32)).astype(BF16)
        g_lo = (g - g_hi.astype(F32) - g_mid.astype(F32)).astype(BF16)
        b = (jnp.dot(tri, g_hi, preferred_element_type=F32)
             + jnp.dot(tri, g_mid, preferred_element_type=F32)
             + jnp.dot(tri, g_lo, preferred_element_type=F32))
        b_last = b[last:last + 1, :]
        q = q_ref[rows, :].astype(F32) * scale
        k = k_ref[rows, :].astype(F32)
        qd_scr[rows, :] = (q * jnp.exp(b)).astype(BF16)
        kd_scr[rows, :] = (k * jnp.exp(-b)).astype(BF16)
        ks_scr[rows, :] = (k * jnp.exp(b_last - b)).astype(BF16)
        decays.append(jnp.exp(b_last))
    scores = [lax.dot_general(qd_scr[rows, :], kd_scr[rows, :], _NT, preferred_element_type=F32)
              for rows in chunk_rows]
    for c, rows in enumerate(chunk_rows):
        u_scr[c] = lax.dot_general(v_ref[rows, :], ks_scr[rows, :], _TN, preferred_element_type=F32)
    for c, rows in enumerate(chunk_rows):
        masked = jnp.where(mask, scores[c], 0.0).astype(BF16)
        oi_scr[rows, :] = jnp.dot(masked, v_ref[rows, :], preferred_element_type=F32)
    st = st_scr[...]
    for c in (reversed(range(n_chunks)) if reverse else range(n_chunks)):
        rows = chunk_rows[c]
        o = oi_scr[rows, :] + lax.dot_general(qd_scr[rows, :], st.astype(BF16), _NT,
                                              preferred_element_type=F32)
        st = decays[c] * st + u_scr[c]
        if finalize:
            o = o + oprev_ref[rows, :].astype(F32)
            o = o * lax.rsqrt(jnp.mean(o * o, axis=-1, keepdims=True) + RMS_EPS)
            r = r_ref[rows, :].astype(F32)
            o = o * ng_ref[...] * (r * jax.nn.sigmoid(r))
        o_ref[rows, :] = o.astype(o_ref.dtype)
    st_scr[...] = st


def _gla_direction(proj, z, wg_pad, b_gate, *, reverse, o_prev=None, norm_g=None, tl=512):
    L = proj.shape[0]
    H = GLA_HEADS
    dk_total = wg_pad.shape[1]
    dk = dk_total // H
    dv_total = (proj.shape[1] - 2 * dk_total) // 2
    dv = dv_total // H
    tl = min(tl, L)
    nb = L // tl
    finalize = o_prev is not None
    blk = (lambda i: nb - 1 - i) if reverse else (lambda i: i)
    k_off = dk_total // dk
    v_off = 2 * dk_total // dv
    r_off = v_off + dv_total // dv

    in_specs = [
        pl.BlockSpec((tl, dk), lambda h, i: (blk(i), h)),
        pl.BlockSpec((tl, dk), lambda h, i: (blk(i), k_off + h)),
        pl.BlockSpec((tl, dv), lambda h, i: (blk(i), v_off + h)),
        pl.BlockSpec((tl, LANES), lambda h, i: (blk(i), 0)),
        pl.BlockSpec((LANES, dk), lambda h, i: (0, h)),
        pl.BlockSpec((1, dk), lambda h, i: (0, h)),
    ]
    args = [proj, proj, proj, z, wg_pad, b_gate.reshape(1, dk_total)]
    if finalize:
        in_specs += [
            pl.BlockSpec((tl, dv), lambda h, i: (blk(i), h)),
            pl.BlockSpec((tl, dv), lambda h, i: (blk(i), r_off + h)),
            pl.BlockSpec((1, dv), lambda h, i: (0, h)),
        ]
        args += [o_prev, proj, norm_g.reshape(1, dv_total)]
    return pl.pallas_call(
        functools.partial(_gla_kernel, reverse=reverse, finalize=finalize,
                          n_chunks=tl // GLA_CHUNK, scale=float(dk) ** -0.5),
        grid=(H, nb),
        in_specs=in_specs,
        out_specs=pl.BlockSpec((tl, dv), lambda h, i: (blk(i), h)),
        out_shape=jax.ShapeDtypeStruct((L, dv_total), BF16),
        scratch_shapes=[pltpu.VMEM((tl, dk), F32), pltpu.VMEM((dv, dk), F32),
                        pltpu.VMEM((tl, dk), BF16), pltpu.VMEM((tl, dk), BF16), pltpu.VMEM((tl, dk), BF16),
                        pltpu.VMEM((tl, dv), F32), pltpu.VMEM((tl // GLA_CHUNK, dv, dk), F32)],
        compiler_params=_params("parallel", "arbitrary"),
        name="gla_backward_scan" if reverse else "gla_forward_scan",
    )(*args)


def _gla_mixer(xb, w_in, layer, w_gate2, b_gate, norm_g):
    dk_total = w_gate2.shape[-1]
    n_main = w_in.shape[2] - 2 * GLA_GATE_RANK
    rank = GLA_GATE_RANK
    proj = _matmul(xb, w_in, layer, n_main, BF16)
    w_z = jnp.zeros((1, w_in.shape[1], LANES), BF16).at[0, :, :2 * rank].set(w_in[layer, :, n_main:])
    z = _matmul(xb, w_z, 0, LANES, F32)
    wg_f = jnp.zeros((LANES, dk_total), F32).at[:rank].set(w_gate2[0])
    wg_b = jnp.zeros((LANES, dk_total), F32).at[rank:2 * rank].set(w_gate2[1])
    o_f = _gla_direction(proj, z, wg_f, b_gate[0], reverse=False)
    return _gla_direction(proj, z, wg_b, b_gate[1], reverse=True, o_prev=o_f, norm_g=norm_g)


def _sgu_kernel(u_ref, v_ref, g_ref, b_ref, ws_ref, bias_ref, o_ref, *, n_chunks, gw):
    C = SGU_CHUNK
    for c in range(n_chunks):
        rows = slice(c * C, (c + 1) * C)
        vn = _layer_norm_rows(v_ref[rows, :].astype(F32), g_ref[...], b_ref[...]).astype(BF16)
        for grp in range(SGU_GROUPS):
            cols = slice(grp * gw, (grp + 1) * gw)
            mixed = jnp.dot(ws_ref[grp], vn[:, cols], preferred_element_type=F32) + bias_ref[:, cols]
            o_ref[rows, cols] = (u_ref[rows, cols].astype(F32) * mixed).astype(BF16)


def _sgu_gate(zact, ln_g, ln_b, w_s, b_s, tl=256):
    L = zact.shape[0]
    half = zact.shape[1] // 2
    gw = half // SGU_GROUPS
    tl = min(tl, L)
    bias = jnp.repeat(jnp.transpose(b_s), gw, axis=1)
    return pl.pallas_call(
        functools.partial(_sgu_kernel, n_chunks=tl // SGU_CHUNK, gw=gw),
        grid=(L // tl,),
        in_specs=[pl.BlockSpec((tl, half), lambda i: (i, 0)),
                  pl.BlockSpec((tl, half), lambda i: (i, 1)),
                  pl.BlockSpec((1, half), lambda i: (0, 0)),
                  pl.BlockSpec((1, half), lambda i: (0, 0)),
                  pl.BlockSpec((SGU_GROUPS, SGU_CHUNK, SGU_CHUNK), lambda i: (0, 0, 0)),
                  pl.BlockSpec((SGU_CHUNK, half), lambda i: (0, 0))],
        out_specs=pl.BlockSpec((tl, half), lambda i: (i, 0)),
        out_shape=jax.ShapeDtypeStruct((L, half), BF16),
        compiler_params=_params("parallel"),
        name="sgu_spatial_gate",
    )(zact, zact, ln_g.reshape(1, half), ln_b.reshape(1, half), w_s.astype(BF16), bias)


ROUTER_ROWS = EXPERTS_PER_GROUP * SUBLANES


def _split_bf16(a):
    hi = a.astype(BF16)
    lo = (a - hi.astype(F32)).astype(BF16)
    return hi, lo


def _router_kernel(x_ref, wt_ref, rb_ref, o_ref):
    xh, xl = _split_bf16(x_ref[...])
    wh, wl = _split_bf16(wt_ref[...])
    logits = (lax.dot_general(wh, xh, _NT, preferred_element_type=F32)
              + lax.dot_general(wh, xl, _NT, preferred_element_type=F32)
              + lax.dot_general(wl, xh, _NT, preferred_element_type=F32))
    scores = jax.nn.sigmoid(logits)
    sel = scores + rb_ref[:, 0:1]
    T = logits.shape[1]
    s_m = [scores[SUBLANES * m:SUBLANES * (m + 1), :] for m in range(EXPERTS_PER_GROUP)]
    a_m = [sel[SUBLANES * m:SUBLANES * (m + 1), :] for m in range(EXPERTS_PER_GROUP)]
    hi1, lo1 = jnp.maximum(a_m[0], a_m[1]), jnp.minimum(a_m[0], a_m[1])
    hi2, lo2 = jnp.maximum(a_m[2], a_m[3]), jnp.minimum(a_m[2], a_m[3])
    group_score = jnp.maximum(hi1, hi2) + jnp.maximum(jnp.minimum(hi1, hi2), jnp.maximum(lo1, lo2))
    gid = lax.broadcasted_iota(jnp.int32, (SUBLANES, T), 0)
    gmax = jnp.max(group_score, axis=0, keepdims=True)
    g_star = jnp.min(jnp.where(group_score == gmax, gid, SUBLANES), axis=0, keepdims=True)
    onehot = gid == g_star
    v_m = [jnp.sum(jnp.where(onehot, a, 0.0), axis=0, keepdims=True) for a in a_m]
    r_m = [jnp.sum(jnp.where(onehot, s, 0.0), axis=0, keepdims=True) for s in s_m]
    best1, idx1, raw1 = v_m[0], jnp.zeros_like(g_star), r_m[0]
    for m in range(1, EXPERTS_PER_GROUP):
        take = v_m[m] > best1
        best1 = jnp.where(take, v_m[m], best1)
        idx1 = jnp.where(take, m, idx1)
        raw1 = jnp.where(take, r_m[m], raw1)
    best2 = jnp.full_like(best1, -jnp.inf)
    idx2, raw2 = jnp.zeros_like(g_star), jnp.zeros_like(raw1)
    for m in range(EXPERTS_PER_GROUP):
        take = jnp.logical_and(idx1 != m, v_m[m] > best2)
        best2 = jnp.where(take, v_m[m], best2)
        idx2 = jnp.where(take, m, idx2)
        raw2 = jnp.where(take, r_m[m], raw2)
    denom = raw1 + raw2
    e1 = (g_star * EXPERTS_PER_GROUP + idx1).astype(F32)
    e2 = (g_star * EXPERTS_PER_GROUP + idx2).astype(F32)
    rid = lax.broadcasted_iota(jnp.int32, (SUBLANES, T), 0)
    out = jnp.where(rid == 0, e1, jnp.where(rid == 1, e2, jnp.where(
        rid == 2, raw1 / denom, jnp.where(rid == 3, raw2 / denom, 0.0))))
    o_ref[...] = out


def _route(x, router_w, router_b, tl=512):
    L, D = x.shape
    tl = min(tl, L)
    w_t = jnp.zeros((EXPERTS_PER_GROUP, SUBLANES, D), F32)
    w_t = w_t.at[:, :N_GROUPS].set(router_w.T.reshape(N_GROUPS, EXPERTS_PER_GROUP, D).transpose(1, 0, 2))
    rb = jnp.full((EXPERTS_PER_GROUP, SUBLANES), NEG_BIG, F32)
    rb = rb.at[:, :N_GROUPS].set(router_b.astype(F32).reshape(N_GROUPS, EXPERTS_PER_GROUP).T)
    rb = jnp.broadcast_to(rb.reshape(ROUTER_ROWS, 1), (ROUTER_ROWS, LANES))
    out = pl.pallas_call(
        _router_kernel,
        grid=(L // tl,),
        in_specs=[pl.BlockSpec((tl, D), lambda i: (i, 0)),
                  pl.BlockSpec((ROUTER_ROWS, D), lambda i: (0, 0)),
                  pl.BlockSpec((ROUTER_ROWS, LANES), lambda i: (0, 0))],
        out_specs=pl.BlockSpec((SUBLANES, tl), lambda i: (0, i)),
        out_shape=jax.ShapeDtypeStruct((SUBLANES, L), F32),
        compiler_params=_params("parallel"),
        name="moe_router",
    )(x, w_t.reshape(ROUTER_ROWS, D), rb)
    expert = out[:TOP_K].astype(jnp.int32)
    gate = out[TOP_K:2 * TOP_K]
    return expert, gate


def _rank_kernel(e_ref, rank_ref, cnt_ref, carry_scr):
    @pl.when(jnp.logical_and(pl.program_id(0) == 0, pl.program_id(1) == 0))
    def _():
        carry_scr[...] = jnp.zeros_like(carry_scr)

    e = e_ref[0]
    T = e.shape[1]
    eid = lax.broadcasted_iota(jnp.int32, (N_EXPERTS, T), 0)
    onehot = eid == e
    oh = jnp.where(onehot, 1.0, 0.0)
    s_idx = lax.broadcasted_iota(jnp.int32, (T, T), 0)
    t_idx = lax.broadcasted_iota(jnp.int32, (T, T), 1)
    before = jnp.where(s_idx < t_idx, 1.0, 0.0).astype(BF16)
    prefix = jnp.dot(oh.astype(BF16), before, preferred_element_type=F32)
    carry = carry_scr[:, 0:1]
    rank_ref[0] = jnp.sum(jnp.where(onehot, prefix + carry, 0.0), axis=0, keepdims=True)
    carry_scr[...] = carry_scr[...] + jnp.sum(oh, axis=1, keepdims=True)
    cnt_ref[...] = carry_scr[...]


def _expert_ranks(expert, tl=512):
    K, L = expert.shape
    tl = min(tl, L)
    rank, cnt = pl.pallas_call(
        _rank_kernel,
        grid=(K, L // tl),
        in_specs=[pl.BlockSpec((1, 1, tl), lambda k, i: (k, 0, i))],
        out_specs=[pl.BlockSpec((1, 1, tl), lambda k, i: (k, 0, i)),
                   pl.BlockSpec((N_EXPERTS, LANES), lambda k, i: (0, 0))],
        out_shape=[jax.ShapeDtypeStruct((K, 1, L), F32),
                   jax.ShapeDtypeStruct((N_EXPERTS, LANES), F32)],
        scratch_shapes=[pltpu.VMEM((N_EXPERTS, LANES), F32)],
        compiler_params=_params("arbitrary", "arbitrary"),
        name="moe_expert_rank",
    )(expert.reshape(K, 1, L))
    return rank.reshape(K, L).astype(jnp.int32), cnt[:, 0].astype(jnp.int32)


DMA_UNROLL = 8


def _slot_token_kernel(dest_ref, tok_ref, *, n_tokens, n_slots):
    def clear(j, carry):
        for u in range(DMA_UNROLL):
            tok_ref[j * DMA_UNROLL + u] = 0
        return carry

    lax.fori_loop(0, n_slots // DMA_UNROLL, clear, 0)
    for k in range(TOP_K):
        def scatter(j, carry):
            for u in range(DMA_UNROLL):
                t = j * DMA_UNROLL + u
                tok_ref[dest_ref[k * n_tokens + t]] = t
            return carry

        lax.fori_loop(0, n_tokens // DMA_UNROLL, scatter, 0)


def _slot_tokens(dest, n_tokens, n_slots):
    return pl.pallas_call(
        functools.partial(_slot_token_kernel, n_tokens=n_tokens, n_slots=n_slots),
        in_specs=[pl.BlockSpec(memory_space=pltpu.SMEM)],
        out_specs=pl.BlockSpec(memory_space=pltpu.SMEM),
        out_shape=jax.ShapeDtypeStruct((n_slots,), jnp.int32),
        name="moe_slot_tokens",
    )(dest)


def _expert_kernel(tok_ref, bexp_ref, nvalid_ref, x_hbm, wg_ref, wu_ref, wd_ref, o_ref, xbuf, sem):
    b = pl.program_id(0)
    n_valid = nvalid_ref[0]
    rows = xbuf.shape[1]

    def row_copy(slot, r, tok):
        return pltpu.make_async_copy(x_hbm.at[pl.ds(tok, 1)], xbuf.at[slot, pl.ds(r, 1)], sem.at[slot])

    def gather(blk, slot):
        base = blk * rows

        def issue(j, carry):
            for u in range(DMA_UNROLL):
                r = j * DMA_UNROLL + u
                row_copy(slot, r, tok_ref[base + r]).start(priority=u % 2)
            return carry

        lax.fori_loop(0, rows // DMA_UNROLL, issue, 0)

    @pl.when(b == 0)
    def _():
        gather(0, 0)

    @pl.when(b + 1 < n_valid)
    def _():
        gather(b + 1, (b + 1) % 2)

    @pl.when(b < n_valid)
    def _():
        slot = b % 2

        def drain(j, carry):
            for u in range(DMA_UNROLL):
                row_copy(slot, 0, 0).wait()
            return carry

        lax.fori_loop(0, rows // DMA_UNROLL, drain, 0)
        x = _unpack_rows(xbuf[slot]).astype(BF16)
        hg = jnp.dot(x, wg_ref[0], preferred_element_type=F32)
        hu = jnp.dot(x, wu_ref[0], preferred_element_type=F32)
        a = (hg * jax.nn.sigmoid(hg) * hu).astype(BF16)
        o_ref[...] = _pack_rows(jnp.dot(a, wd_ref[0], preferred_element_type=F32))

    @pl.when(b >= n_valid)
    def _():
        o_ref[...] = jnp.zeros_like(o_ref)


def _expert_ffn(xp, slot_tok, w_gate, w_up, w_down, layer, blk_expert, n_valid):
    D, F = w_gate.shape[2:]
    P = slot_tok.shape[0]
    nb = P // DISPATCH_BLOCK
    grid_spec = pltpu.PrefetchScalarGridSpec(
        num_scalar_prefetch=3,
        grid=(nb,),
        in_specs=[pl.BlockSpec(memory_space=pl.ANY),
                  pl.BlockSpec((None, 1, D, F), lambda b, tok, be, nv: (layer, be[b], 0, 0)),
                  pl.BlockSpec((None, 1, D, F), lambda b, tok, be, nv: (layer, be[b], 0, 0)),
                  pl.BlockSpec((None, 1, F, D), lambda b, tok, be, nv: (layer, be[b], 0, 0))],
        out_specs=pl.BlockSpec((DISPATCH_BLOCK, D // 2), lambda b, tok, be, nv: (b, 0)),
        scratch_shapes=[pltpu.VMEM((2, DISPATCH_BLOCK, D // 2), jnp.uint32), pltpu.SemaphoreType.DMA((2,))],
    )
    return pl.pallas_call(
        _expert_kernel,
        grid_spec=grid_spec,
        out_shape=jax.ShapeDtypeStruct((P, D // 2), jnp.uint32),
        compiler_params=_params("arbitrary"),
        name="moe_expert_ffn",
    )(slot_tok, blk_expert, n_valid, xp, w_gate, w_up, w_down)


COMBINE_TOKENS = 256


def _combine_kernel(dest_ref, ys_hbm, x_ref, gate_ref, g_ref, b_ref, of_ref, ob_ref, buf, sem,
                    *, n_tokens, tb, alpha):
    i = pl.program_id(0)

    def row_copy(half, k, r, src):
        return pltpu.make_async_copy(ys_hbm.at[pl.ds(src, 1)], buf.at[half, k, pl.ds(r, 1)], sem.at[half])

    def gather(step, half):
        base = step * tb

        def issue(j, carry):
            for u in range(DMA_UNROLL):
                r = j * DMA_UNROLL + u
                for k in range(TOP_K):
                    row_copy(half, k, r, dest_ref[k * n_tokens + base + r]).start(priority=k % 2)
            return carry

        lax.fori_loop(0, tb // DMA_UNROLL, issue, 0)

    @pl.when(i == 0)
    def _():
        gather(0, 0)

    @pl.when(i + 1 < pl.num_programs(0))
    def _():
        gather(i + 1, (i + 1) % 2)

    half = i % 2

    def drain(j, carry):
        for u in range(DMA_UNROLL * TOP_K):
            row_copy(half, 0, 0, 0).wait()
        return carry

    lax.fori_loop(0, tb // DMA_UNROLL, drain, 0)

    g = g_ref[...]
    b = b_ref[...]

    def body(r, carry):
        rows = pl.ds(pl.multiple_of(r * LN_ROWS, LN_ROWS), LN_ROWS)
        gates = gate_ref[rows, :]
        m = _unpack_rows(buf[half, 0, rows, :]) * gates[:, 0:1]
        for k in range(1, TOP_K):
            m = m + _unpack_rows(buf[half, k, rows, :]) * gates[:, k:k + 1]
        out = _layer_norm_rows(alpha * x_ref[rows, :] + m, g, b)
        of_ref[rows, :] = out
        ob_ref[rows, :] = out.astype(BF16)
        return carry

    lax.fori_loop(0, tb // LN_ROWS, body, 0)


def _combine(ys, x, dest, gate_cols, g, b, alpha):
    L, D = x.shape
    tb = min(COMBINE_TOKENS, L)
    grid_spec = pltpu.PrefetchScalarGridSpec(
        num_scalar_prefetch=1,
        grid=(L // tb,),
        in_specs=[pl.BlockSpec(memory_space=pl.ANY),
                  pl.BlockSpec((tb, D), lambda i, d: (i, 0)),
                  pl.BlockSpec((tb, TOP_K), lambda i, d: (i, 0)),
                  pl.BlockSpec((1, D), lambda i, d: (0, 0)),
                  pl.BlockSpec((1, D), lambda i, d: (0, 0))],
        out_specs=[pl.BlockSpec((tb, D), lambda i, d: (i, 0)),
                   pl.BlockSpec((tb, D), lambda i, d: (i, 0))],
        scratch_shapes=[pltpu.VMEM((2, TOP_K, tb, D // 2), ys.dtype), pltpu.SemaphoreType.DMA((2,))],
    )
    return pl.pallas_call(
        functools.partial(_combine_kernel, n_tokens=L, tb=tb, alpha=alpha),
        grid_spec=grid_spec,
        out_shape=[jax.ShapeDtypeStruct((L, D), F32), jax.ShapeDtypeStruct((L, D), BF16)],
        compiler_params=_params("arbitrary"),
        name="moe_combine_layernorm",
    )(dest, ys, x, gate_cols, g.reshape(1, D), b.reshape(1, D))


def _moe_layer(x, xp, router_w, router_b, w_gate, w_up, w_down, layer, g, b, alpha):
    L, D = x.shape
    blk = DISPATCH_BLOCK
    expert, gate = _route(x, router_w, router_b)
    rank, counts = _expert_ranks(expert)
    padded = (counts + blk - 1) // blk * blk
    pends = jnp.cumsum(padded)
    pstarts = pends - padded
    nb = (L * TOP_K) // blk + N_EXPERTS
    n_valid = (pends[-1] // blk).astype(jnp.int32)
    dest = (jnp.sum(jnp.where(expert[None] == jnp.arange(N_EXPERTS)[:, None, None],
                              pstarts[:, None, None], 0), axis=0) + rank).astype(jnp.int32)
    blk_idx = jnp.minimum(jnp.arange(nb, dtype=jnp.int32), n_valid - 1)
    blk_expert = jnp.minimum(jnp.sum(pends[None, :] <= (blk_idx * blk)[:, None], axis=1),
                             N_EXPERTS - 1).astype(jnp.int32)
    dest_flat = dest.reshape(TOP_K * L)
    slot_tok = _slot_tokens(dest_flat, L, nb * blk)
    ys = _expert_ffn(xp, slot_tok, w_gate, w_up, w_down, layer, blk_expert, n_valid.reshape(1))
    return _combine(ys, x, dest_flat, jnp.transpose(gate), g, b, alpha)


def kernel(x, gla_w_in, gla_w_gate2, gla_b_gate, gla_norm_g, gla_w_out, sgu_w_in, sgu_ln_g, sgu_ln_b,
           sgu_w_s, sgu_b_s, sgu_w_out, router_w, router_b, moe_w_gate, moe_w_up, moe_w_down, ln_g, ln_b):
    B, L, D = x.shape
    depth = ln_g.shape[0]
    alpha = float((2 * depth) ** 0.25)
    gla_w_in, gla_w_out, sgu_w_in, sgu_w_out, moe_w_gate, moe_w_up, moe_w_down = (
        w.astype(BF16) for w in (gla_w_in, gla_w_out, sgu_w_in, sgu_w_out, moe_w_gate, moe_w_up, moe_w_down))
    outs = []
    for bi in range(B):
        xf = x[bi]
        xb = xf.astype(BF16)
        for i in range(depth):
            j = i // 2
            if i % 2 == 0:
                a = _gla_mixer(xb, gla_w_in, j, gla_w_gate2[j], gla_b_gate[j], gla_norm_g[j])
                w_out = gla_w_out
            else:
                zact = _matmul(xb, sgu_w_in, j, sgu_w_in.shape[2], BF16, act="gelu")
                a = _sgu_gate(zact, sgu_ln_g[j], sgu_ln_b[j], sgu_w_s[j], sgu_b_s[j])
                w_out = sgu_w_out
            xf, xp = _matmul_res_ln(a, w_out, j, xf, ln_g[i, 0], ln_b[i, 0], alpha)
            xf, xb = _moe_layer(xf, xp, router_w, router_b, moe_w_gate, moe_w_up, moe_w_down, i,
                                ln_g[i, 1], ln_b[i, 1], alpha)
        outs.append(xf)
    return jnp.stack(outs, axis=0)
```

```python
import functools

import jax
import jax.numpy as jnp
from jax import lax
from jax.experimental import pallas as pl
from jax.experimental.pallas import tpu as pltpu

GLA_HEADS = 8
GLA_GATE_RANK = 16
GLA_TAU = 16.0
GLA_CHUNK = 64
SGU_GROUPS = 8
SGU_CHUNK = 128
N_EXPERTS = 16
N_GROUPS = 4
EXPERTS_PER_GROUP = N_EXPERTS // N_GROUPS
TOP_K = 2
LN_EPS = 1e-5
RMS_EPS = 1e-6

LANES = 128
SUBLANES = 8
VMEM_LIMIT_BYTES = 56 * 1024 * 1024

DISPATCH_BLOCK = 256
NEG_BIG = -1e30

F32 = jnp.float32
BF16 = jnp.bfloat16
_NT = (((1,), (1,)), ((), ()))
_TN = (((0,), (0,)), ((), ()))


def _params(*sem):
    return pltpu.CompilerParams(dimension_semantics=sem, vmem_limit_bytes=VMEM_LIMIT_BYTES)


def _layer_norm_rows(y, g, b):
    mu = jnp.mean(y, axis=-1, keepdims=True)
    yc = y - mu
    var = jnp.mean(yc * yc, axis=-1, keepdims=True)
    return yc * lax.rsqrt(var + LN_EPS) * g + b


def _mm_kernel(x_ref, w_ref, o_ref, *, act):
    acc = jnp.dot(x_ref[...], w_ref[...], preferred_element_type=F32)
    if act == "gelu":
        acc = 0.5 * acc * (1.0 + lax.erf(acc * (2.0 ** -0.5)))
    o_ref[...] = acc.astype(o_ref.dtype)


def _tile(dim, target, align):
    t = min(target, dim) // align * align
    while dim % t:
        t -= align
    return t


def _matmul(x, w, layer, n, out_dtype, act=None, tm=1024, tn=1024):
    m, k = x.shape
    tm, tn = _tile(m, tm, SUBLANES), _tile(n, tn, LANES)
    return pl.pallas_call(
        functools.partial(_mm_kernel, act=act),
        grid=(m // tm, n // tn),
        in_specs=[pl.BlockSpec((tm, k), lambda i, j: (i, 0)),
                  pl.BlockSpec((None, k, tn), lambda i, j: (layer, 0, j))],
        out_specs=pl.BlockSpec((tm, tn), lambda i, j: (i, j)),
        out_shape=jax.ShapeDtypeStruct((m, n), out_dtype),
        compiler_params=_params("parallel", "arbitrary"),
        name="dense_matmul",
    )(x, w)


PACK_GROUP = 512
_HI_MASK = 0xFFFF0000


def _bf16_bits(x):
    return lax.bitcast_convert_type(x.astype(BF16).astype(F32), jnp.uint32)


def _pack_rows(x):
    half = PACK_GROUP // 2
    words = []
    for c0 in range(0, x.shape[1], PACK_GROUP):
        lo = _bf16_bits(x[:, c0:c0 + half]) >> 16
        hi = _bf16_bits(x[:, c0 + half:c0 + PACK_GROUP]) & jnp.uint32(_HI_MASK)
        words.append(hi | lo)
    return words[0] if len(words) == 1 else jnp.concatenate(words, axis=1)


def _unpack_rows(w):
    half = PACK_GROUP // 2
    cols = []
    for c0 in range(0, w.shape[1], half):
        piece = w[:, c0:c0 + half]
        cols.append(lax.bitcast_convert_type(piece << 16, F32))
        cols.append(lax.bitcast_convert_type(piece & jnp.uint32(_HI_MASK), F32))
    return jnp.concatenate(cols, axis=1)


LN_ROWS = 32


def _mm_ln_kernel(a_ref, w_ref, res_ref, g_ref, b_ref, of_ref, xp_ref, y_scr, mu_scr, rs_scr,
                  *, alpha, nj, tm, n):
    j = pl.program_id(1)

    @pl.when(j < nj)
    def _():
        y_scr[j] = alpha * res_ref[...] + jnp.dot(a_ref[...], w_ref[...], preferred_element_type=F32)

    @pl.when(j == nj)
    def _():
        def body(r, carry):
            rows = pl.ds(pl.multiple_of(r * LN_ROWS, LN_ROWS), LN_ROWS)
            total = jnp.sum(y_scr[0, rows, :], axis=-1, keepdims=True)
            for t in range(1, nj):
                total = total + jnp.sum(y_scr[t, rows, :], axis=-1, keepdims=True)
            mu = total * (1.0 / n)
            sq = jnp.zeros_like(mu)
            for t in range(nj):
                yc = y_scr[t, rows, :] - mu
                sq = sq + jnp.sum(yc * yc, axis=-1, keepdims=True)
            mu_scr[rows, :] = mu
            rs_scr[rows, :] = lax.rsqrt(sq * (1.0 / n) + LN_EPS)
            return carry

        lax.fori_loop(0, tm // LN_ROWS, body, 0, unroll=4)

    @pl.when(j >= nj)
    def _():
        t = j - nj
        g = g_ref[...]
        b = b_ref[...]

        def body(r, carry):
            rows = pl.ds(pl.multiple_of(r * LN_ROWS, LN_ROWS), LN_ROWS)
            out = (y_scr[t, rows, :] - mu_scr[rows, :]) * rs_scr[rows, :] * g + b
            of_ref[rows, :] = out
            xp_ref[rows, :] = _pack_rows(out)
            return carry

        lax.fori_loop(0, tm // LN_ROWS, body, 0, unroll=8)


def _matmul_res_ln(a, w, layer, res, g, b, alpha, tm=1024, tn=512):
    m, kdim = a.shape
    n = w.shape[2]
    tm, tn = _tile(m, tm, SUBLANES), _tile(n, tn, PACK_GROUP)
    nj = n // tn
    fill = lambda j: jnp.minimum(j, nj - 1)
    emit = lambda j: jnp.maximum(j - nj, 0)
    return pl.pallas_call(
        functools.partial(_mm_ln_kernel, alpha=alpha, nj=nj, tm=tm, n=n),
        grid=(m // tm, 2 * nj),
        in_specs=[pl.BlockSpec((tm, kdim), lambda i, j: (i, 0)),
                  pl.BlockSpec((None, kdim, tn), lambda i, j: (layer, 0, fill(j))),
                  pl.BlockSpec((tm, tn), lambda i, j: (i, fill(j))),
                  pl.BlockSpec((1, tn), lambda i, j: (0, emit(j))),
                  pl.BlockSpec((1, tn), lambda i, j: (0, emit(j)))],
        out_specs=[pl.BlockSpec((tm, tn), lambda i, j: (i, emit(j))),
                   pl.BlockSpec((tm, tn // 2), lambda i, j: (i, emit(j)))],
        out_shape=[jax.ShapeDtypeStruct((m, n), F32),
                   jax.ShapeDtypeStruct((m, n // 2), jnp.uint32)],
        scratch_shapes=[pltpu.VMEM((nj, tm, tn), F32), pltpu.VMEM((tm, 1), F32), pltpu.VMEM((tm, 1), F32)],
        compiler_params=_params("parallel", "arbitrary"),
        name="matmul_residual_layernorm",
    )(a, w, res, g.reshape(1, n), b.reshape(1, n))


def _log_sigmoid(x):
    return jnp.minimum(x, 0.0) - jnp.log1p(jnp.exp(-jnp.abs(x)))


def _gla_kernel(*refs, reverse, finalize, n_chunks, scale):
    if finalize:
        (q_ref, k_ref, z_ref, qn_ref, kn_ref, zn_ref, v_ref, wg_ref, bg_ref, oprev_ref, r_ref, ng_ref,
         o_ref, g_scr, st_scr, oi_scr, u_scr, *prep) = refs
    else:
        (q_ref, k_ref, z_ref, qn_ref, kn_ref, zn_ref, v_ref, wg_ref, bg_ref,
         o_ref, g_scr, st_scr, oi_scr, u_scr, *prep) = refs
    set_a, set_b = prep[:4], prep[4:]
    C = GLA_CHUNK
    i = pl.program_id(1)

    row = lax.broadcasted_iota(jnp.int32, (C, C), 0)
    col = lax.broadcasted_iota(jnp.int32, (C, C), 1)
    if reverse:
        tri = jnp.where(col >= row, 1.0, 0.0).astype(BF16)
        mask = col > row
        last = 0
    else:
        tri = jnp.where(col <= row, 1.0, 0.0).astype(BF16)
        mask = col <= row
        last = C - 1

    chunk_rows = [slice(c * C, (c + 1) * C) for c in range(n_chunks)]

    def prepare_logits(zr):
        zh, zl = _split_bf16(zr[...])
        wh, wl = _split_bf16(wg_ref[...])
        return (jnp.dot(zh, wh, preferred_element_type=F32) + jnp.dot(zh, wl, preferred_element_type=F32)
                + jnp.dot(zl, wh, preferred_element_type=F32)) + bg_ref[...]

    def prepare_cumsums(logit):
        g_scr[...] = _log_sigmoid(logit) * (1.0 / GLA_TAU)
        sums = []
        for rows in chunk_rows:
            g_hi, g_lo = _split_bf16(g_scr[rows, :])
            sums.append(jnp.dot(tri, g_hi, preferred_element_type=F32)
                        + jnp.dot(tri, g_lo, preferred_element_type=F32))
        return sums

    def prepare_store(qr, kr, sums, dst):
        qd_scr, kd_scr, ks_scr, dec_scr = dst
        for c, rows in enumerate(chunk_rows):
            b = sums[c]
            b_last = b[last:last + 1, :]
            q = qr[rows, :].astype(F32) * scale
            k = kr[rows, :].astype(F32)
            qd_scr[rows, :] = (q * jnp.exp(b)).astype(BF16)
            kd_scr[rows, :] = (k * jnp.exp(-b)).astype(BF16)
            ks_scr[rows, :] = (k * jnp.exp(b_last - b)).astype(BF16)
            dec_scr[c:c + 1, :] = jnp.exp(b_last)

    def attend_scores(src):
        qd_scr, kd_scr, ks_scr, _ = src
        scores = [lax.dot_general(qd_scr[rows, :], kd_scr[rows, :], _NT, preferred_element_type=F32)
                  for rows in chunk_rows]
        for c, rows in enumerate(chunk_rows):
            u_scr[c] = lax.dot_general(v_ref[rows, :], ks_scr[rows, :], _TN, preferred_element_type=F32)
        return scores

    def attend_outputs(src, scores):
        qd_scr, _, _, dec_scr = src
        for c, rows in enumerate(chunk_rows):
            masked = jnp.where(mask, scores[c], 0.0).astype(BF16)
            oi_scr[rows, :] = jnp.dot(masked, v_ref[rows, :], preferred_element_type=F32)
        st = st_scr[...]
        for c in (reversed(range(n_chunks)) if reverse else range(n_chunks)):
            rows = chunk_rows[c]
            o = oi_scr[rows, :] + lax.dot_general(qd_scr[rows, :], st.astype(BF16), _NT,
                                                  preferred_element_type=F32)
            st = dec_scr[c:c + 1, :] * st + u_scr[c]
            if finalize:
                o = o + oprev_ref[rows, :].astype(F32)
                o = o * lax.rsqrt(jnp.mean(o * o, axis=-1, keepdims=True) + RMS_EPS)
                r = r_ref[rows, :].astype(F32)
                o = o * ng_ref[...] * (r * jax.nn.sigmoid(r))
            o_ref[rows, :] = o.astype(o_ref.dtype)
        st_scr[...] = st

    def step(cur, nxt):
        logit = prepare_logits(zn_ref)
        scores = attend_scores(cur)
        sums = prepare_cumsums(logit)
        attend_outputs(cur, scores)
        prepare_store(qn_ref, kn_ref, sums, nxt)

    @pl.when(i == 0)
    def _():
        st_scr[...] = jnp.zeros_like(st_scr)
        prepare_store(q_ref, k_ref, prepare_cumsums(prepare_logits(z_ref)), set_a)

    @pl.when(i % 2 == 0)
    def _():
        step(set_a, set_b)

    @pl.when(i % 2 == 1)
    def _():
        step(set_b, set_a)


def _gla_direction(proj, z, wg_pad, b_gate, *, reverse, o_prev=None, norm_g=None, tl=512):
    L = proj.shape[0]
    H = GLA_HEADS
    dk_total = wg_pad.shape[1]
    dk = dk_total // H
    dv_total = (proj.shape[1] - 2 * dk_total) // 2
    dv = dv_total // H
    tl = min(tl, L)
    nb = L // tl
    n_chunks = tl // GLA_CHUNK
    finalize = o_prev is not None
    blk = (lambda i: nb - 1 - i) if reverse else (lambda i: i)
    nxt = lambda i: blk(jnp.minimum(i + 1, nb - 1))
    k_off = dk_total // dk
    v_off = 2 * dk_total // dv
    r_off = v_off + dv_total // dv

    in_specs = [
        pl.BlockSpec((tl, dk), lambda h, i: (blk(i), h)),
        pl.BlockSpec((tl, dk), lambda h, i: (blk(i), k_off + h)),
        pl.BlockSpec((tl, LANES), lambda h, i: (blk(i), 0)),
        pl.BlockSpec((tl, dk), lambda h, i: (nxt(i), h)),
        pl.BlockSpec((tl, dk), lambda h, i: (nxt(i), k_off + h)),
        pl.BlockSpec((tl, LANES), lambda h, i: (nxt(i), 0)),
        pl.BlockSpec((tl, dv), lambda h, i: (blk(i), v_off + h)),
        pl.BlockSpec((LANES, dk), lambda h, i: (0, h)),
        pl.BlockSpec((1, dk), lambda h, i: (0, h)),
    ]
    args = [proj, proj, z, proj, proj, z, proj, wg_pad, b_gate.reshape(1, dk_total)]
    if finalize:
        in_specs += [
            pl.BlockSpec((tl, dv), lambda h, i: (blk(i), h)),
            pl.BlockSpec((tl, dv), lambda h, i: (blk(i), r_off + h)),
            pl.BlockSpec((1, dv), lambda h, i: (0, h)),
        ]
        args += [o_prev, proj, norm_g.reshape(1, dv_total)]
    prep_set = [pltpu.VMEM((tl, dk), BF16), pltpu.VMEM((tl, dk), BF16), pltpu.VMEM((tl, dk), BF16),
                pltpu.VMEM((n_chunks, dk), F32)]
    return pl.pallas_call(
        functools.partial(_gla_kernel, reverse=reverse, finalize=finalize,
                          n_chunks=n_chunks, scale=float(dk) ** -0.5),
        grid=(H, nb),
        in_specs=in_specs,
        out_specs=pl.BlockSpec((tl, dv), lambda h, i: (blk(i), h)),
        out_shape=jax.ShapeDtypeStruct((L, dv_total), BF16),
        scratch_shapes=[pltpu.VMEM((tl, dk), F32), pltpu.VMEM((dv, dk), F32),
                        pltpu.VMEM((tl, dv), F32), pltpu.VMEM((n_chunks, dv, dk), F32)] + prep_set + prep_set,
        compiler_params=_params("parallel", "arbitrary"),
        name="gla_backward_scan" if reverse else "gla_forward_scan",
    )(*args)


def _gla_mixer(xb, w_in, layer, w_gate2, b_gate, norm_g):
    dk_total = w_gate2.shape[-1]
    n_main = w_in.shape[2] - 2 * GLA_GATE_RANK
    rank = GLA_GATE_RANK
    proj = _matmul(xb, w_in, layer, n_main, BF16)
    w_z = jnp.zeros((1, w_in.shape[1], LANES), BF16).at[0, :, :2 * rank].set(w_in[layer, :, n_main:])
    z = _matmul(xb, w_z, 0, LANES, F32)
    wg_f = jnp.zeros((LANES, dk_total), F32).at[:rank].set(w_gate2[0])
    wg_b = jnp.zeros((LANES, dk_total), F32).at[rank:2 * rank].set(w_gate2[1])
    o_f = _gla_direction(proj, z, wg_f, b_gate[0], reverse=False)
    return _gla_direction(proj, z, wg_b, b_gate[1], reverse=True, o_prev=o_f, norm_g=norm_g)


def _sgu_kernel(u_ref, v_ref, g_ref, b_ref, ws_ref, bias_ref, o_ref, *, n_chunks, gw):
    C = SGU_CHUNK
    for c in range(n_chunks):
        rows = slice(c * C, (c + 1) * C)
        vn = _layer_norm_rows(v_ref[rows, :].astype(F32), g_ref[...], b_ref[...]).astype(BF16)
        for grp in range(SGU_GROUPS):
            cols = slice(grp * gw, (grp + 1) * gw)
            mixed = jnp.dot(ws_ref[grp], vn[:, cols], preferred_element_type=F32) + bias_ref[:, cols]
            o_ref[rows, cols] = (u_ref[rows, cols].astype(F32) * mixed).astype(BF16)


def _sgu_gate(zact, ln_g, ln_b, w_s, b_s, tl=256):
    L = zact.shape[0]
    half = zact.shape[1] // 2
    gw = half // SGU_GROUPS
    tl = min(tl, L)
    bias = jnp.repeat(jnp.transpose(b_s), gw, axis=1)
    return pl.pallas_call(
        functools.partial(_sgu_kernel, n_chunks=tl // SGU_CHUNK, gw=gw),
        grid=(L // tl,),
        in_specs=[pl.BlockSpec((tl, half), lambda i: (i, 0)),
                  pl.BlockSpec((tl, half), lambda i: (i, 1)),
                  pl.BlockSpec((1, half), lambda i: (0, 0)),
                  pl.BlockSpec((1, half), lambda i: (0, 0)),
                  pl.BlockSpec((SGU_GROUPS, SGU_CHUNK, SGU_CHUNK), lambda i: (0, 0, 0)),
                  pl.BlockSpec((SGU_CHUNK, half), lambda i: (0, 0))],
        out_specs=pl.BlockSpec((tl, half), lambda i: (i, 0)),
        out_shape=jax.ShapeDtypeStruct((L, half), BF16),
        compiler_params=_params("parallel"),
        name="sgu_spatial_gate",
    )(zact, zact, ln_g.reshape(1, half), ln_b.reshape(1, half), w_s.astype(BF16), bias)


ROUTER_ROWS = EXPERTS_PER_GROUP * SUBLANES


def _split_bf16(a):
    hi = a.astype(BF16)
    lo = (a - hi.astype(F32)).astype(BF16)
    return hi, lo


def _router_kernel(x_ref, wt_ref, rb_ref, o_ref):
    xh, xl = _split_bf16(x_ref[...])
    wh, wl = _split_bf16(wt_ref[...])
    logits = (lax.dot_general(wh, xh, _NT, preferred_element_type=F32)
              + lax.dot_general(wh, xl, _NT, preferred_element_type=F32)
              + lax.dot_general(wl, xh, _NT, preferred_element_type=F32))
    scores = jax.nn.sigmoid(logits)
    sel = scores + rb_ref[:, 0:1]
    T = logits.shape[1]
    s_m = [scores[SUBLANES * m:SUBLANES * (m + 1), :] for m in range(EXPERTS_PER_GROUP)]
    a_m = [sel[SUBLANES * m:SUBLANES * (m + 1), :] for m in range(EXPERTS_PER_GROUP)]
    hi1, lo1 = jnp.maximum(a_m[0], a_m[1]), jnp.minimum(a_m[0], a_m[1])
    hi2, lo2 = jnp.maximum(a_m[2], a_m[3]), jnp.minimum(a_m[2], a_m[3])
    group_score = jnp.maximum(hi1, hi2) + jnp.maximum(jnp.minimum(hi1, hi2), jnp.maximum(lo1, lo2))
    gid = lax.broadcasted_iota(jnp.int32, (SUBLANES, T), 0)
    gmax = jnp.max(group_score, axis=0, keepdims=True)
    g_star = jnp.min(jnp.where(group_score == gmax, gid, SUBLANES), axis=0, keepdims=True)
    onehot = gid == g_star
    v_m = [jnp.sum(jnp.where(onehot, a, 0.0), axis=0, keepdims=True) for a in a_m]
    r_m = [jnp.sum(jnp.where(onehot, s, 0.0), axis=0, keepdims=True) for s in s_m]
    best1, idx1, raw1 = v_m[0], jnp.zeros_like(g_star), r_m[0]
    for m in range(1, EXPERTS_PER_GROUP):
        take = v_m[m] > best1
        best1 = jnp.where(take, v_m[m], best1)
        idx1 = jnp.where(take, m, idx1)
        raw1 = jnp.where(take, r_m[m], raw1)
    best2 = jnp.full_like(best1, -jnp.inf)
    idx2, raw2 = jnp.zeros_like(g_star), jnp.zeros_like(raw1)
    for m in range(EXPERTS_PER_GROUP):
        take = jnp.logical_and(idx1 != m, v_m[m] > best2)
        best2 = jnp.where(take, v_m[m], best2)
        idx2 = jnp.where(take, m, idx2)
        raw2 = jnp.where(take, r_m[m], raw2)
    denom = raw1 + raw2
    e1 = (g_star * EXPERTS_PER_GROUP + idx1).astype(F32)
    e2 = (g_star * EXPERTS_PER_GROUP + idx2).astype(F32)
    rid = lax.broadcasted_iota(jnp.int32, (SUBLANES, T), 0)
    out = jnp.where(rid == 0, e1, jnp.where(rid == 1, e2, jnp.where(
        rid == 2, raw1 / denom, jnp.where(rid == 3, raw2 / denom, 0.0))))
    o_ref[...] = out


def _route(x, router_w, router_b, tl=512):
    L, D = x.shape
    tl = min(tl, L)
    w_t = jnp.zeros((EXPERTS_PER_GROUP, SUBLANES, D), F32)
    w_t = w_t.at[:, :N_GROUPS].set(router_w.T.reshape(N_GROUPS, EXPERTS_PER_GROUP, D).transpose(1, 0, 2))
    rb = jnp.full((EXPERTS_PER_GROUP, SUBLANES), NEG_BIG, F32)
    rb = rb.at[:, :N_GROUPS].set(router_b.astype(F32).reshape(N_GROUPS, EXPERTS_PER_GROUP).T)
    rb = jnp.broadcast_to(rb.reshape(ROUTER_ROWS, 1), (ROUTER_ROWS, LANES))
    out = pl.pallas_call(
        _router_kernel,
        grid=(L // tl,),
        in_specs=[pl.BlockSpec((tl, D), lambda i: (i, 0)),
                  pl.BlockSpec((ROUTER_ROWS, D), lambda i: (0, 0)),
                  pl.BlockSpec((ROUTER_ROWS, LANES), lambda i: (0, 0))],
        out_specs=pl.BlockSpec((SUBLANES, tl), lambda i: (0, i)),
        out_shape=jax.ShapeDtypeStruct((SUBLANES, L), F32),
        compiler_params=_params("parallel"),
        name="moe_router",
    )(x, w_t.reshape(ROUTER_ROWS, D), rb)
    expert = out[:TOP_K].astype(jnp.int32)
    gate = out[TOP_K:2 * TOP_K]
    return expert, gate


def _rank_kernel(e_ref, rank_ref, cnt_ref, carry_scr):
    @pl.when(jnp.logical_and(pl.program_id(0) == 0, pl.program_id(1) == 0))
    def _():
        carry_scr[...] = jnp.zeros_like(carry_scr)

    e = e_ref[0]
    T = e.shape[1]
    eid = lax.broadcasted_iota(jnp.int32, (N_EXPERTS, T), 0)
    onehot = eid == e
    oh = jnp.where(onehot, 1.0, 0.0)
    s_idx = lax.broadcasted_iota(jnp.int32, (T, T), 0)
    t_idx = lax.broadcasted_iota(jnp.int32, (T, T), 1)
    before = jnp.where(s_idx < t_idx, 1.0, 0.0).astype(BF16)
    prefix = jnp.dot(oh.astype(BF16), before, preferred_element_type=F32)
    carry = carry_scr[:, 0:1]
    rank_ref[0] = jnp.sum(jnp.where(onehot, prefix + carry, 0.0), axis=0, keepdims=True)
    carry_scr[...] = carry_scr[...] + jnp.sum(oh, axis=1, keepdims=True)
    cnt_ref[...] = carry_scr[...]


def _expert_ranks(expert, tl=512):
    K, L = expert.shape
    tl = min(tl, L)
    rank, cnt = pl.pallas_call(
        _rank_kernel,
        grid=(K, L // tl),
        in_specs=[pl.BlockSpec((1, 1, tl), lambda k, i: (k, 0, i))],
        out_specs=[pl.BlockSpec((1, 1, tl), lambda k, i: (k, 0, i)),
                   pl.BlockSpec((N_EXPERTS, LANES), lambda k, i: (0, 0))],
        out_shape=[jax.ShapeDtypeStruct((K, 1, L), F32),
                   jax.ShapeDtypeStruct((N_EXPERTS, LANES), F32)],
        scratch_shapes=[pltpu.VMEM((N_EXPERTS, LANES), F32)],
        compiler_params=_params("arbitrary", "arbitrary"),
        name="moe_expert_rank",
    )(expert.reshape(K, 1, L))
    return rank.reshape(K, L).astype(jnp.int32), cnt[:, 0].astype(jnp.int32)


DMA_UNROLL = 8


def _slot_token_kernel(dest_ref, tok_ref, *, n_tokens, n_slots):
    def clear(j, carry):
        for u in range(DMA_UNROLL):
            tok_ref[j * DMA_UNROLL + u] = 0
        return carry

    lax.fori_loop(0, n_slots // DMA_UNROLL, clear, 0)
    for k in range(TOP_K):
        def scatter(j, carry):
            for u in range(DMA_UNROLL):
                t = j * DMA_UNROLL + u
                tok_ref[dest_ref[k * n_tokens + t]] = t
            return carry

        lax.fori_loop(0, n_tokens // DMA_UNROLL, scatter, 0)


def _slot_tokens(dest, n_tokens, n_slots):
    return pl.pallas_call(
        functools.partial(_slot_token_kernel, n_tokens=n_tokens, n_slots=n_slots),
        in_specs=[pl.BlockSpec(memory_space=pltpu.SMEM)],
        out_specs=pl.BlockSpec(memory_space=pltpu.SMEM),
        out_shape=jax.ShapeDtypeStruct((n_slots,), jnp.int32),
        name="moe_slot_tokens",
    )(dest)


def _expert_kernel(tok_ref, bexp_ref, nvalid_ref, x_hbm, wg_ref, wu_ref, wd_ref, o_ref, xbuf, sem):
    b = pl.program_id(0)
    n_valid = nvalid_ref[0]
    rows = xbuf.shape[1]

    def row_copy(slot, r, tok):
        return pltpu.make_async_copy(x_hbm.at[pl.ds(tok, 1)], xbuf.at[slot, pl.ds(r, 1)], sem.at[slot])

    def drain(slot):
        def body(j, carry):
            for u in range(DMA_UNROLL):
                row_copy(slot, 0, 0).wait()
            return carry

        lax.fori_loop(0, rows // DMA_UNROLL, body, 0)

    @pl.when(b == 0)
    def _():
        def issue(j, carry):
            for u in range(DMA_UNROLL):
                r = j * DMA_UNROLL + u
                row_copy(0, r, tok_ref[r]).start(priority=u % 2)
            return carry

        lax.fori_loop(0, rows // DMA_UNROLL, issue, 0)

    @pl.when(b < n_valid)
    def _():
        slot = b % 2
        drain(slot)
        base = jnp.minimum(b + 1, n_valid - 1) * rows
        for r in range(rows):
            row_copy(1 - slot, r, tok_ref[base + r]).start(priority=r % 2)
        x = _unpack_rows(xbuf[slot]).astype(BF16)
        hg = jnp.dot(x, wg_ref[0], preferred_element_type=F32)
        hu = jnp.dot(x, wu_ref[0], preferred_element_type=F32)
        a = (hg * jax.nn.sigmoid(hg) * hu).astype(BF16)
        o_ref[...] = _pack_rows(jnp.dot(a, wd_ref[0], preferred_element_type=F32))

    @pl.when(b == n_valid - 1)
    def _():
        drain(1 - b % 2)

    @pl.when(b >= n_valid)
    def _():
        o_ref[...] = jnp.zeros_like(o_ref)


def _expert_ffn(xp, slot_tok, w_gate, w_up, w_down, layer, blk_expert, n_valid):
    D, F = w_gate.shape[2:]
    P = slot_tok.shape[0]
    nb = P // DISPATCH_BLOCK
    grid_spec = pltpu.PrefetchScalarGridSpec(
        num_scalar_prefetch=3,
        grid=(nb,),
        in_specs=[pl.BlockSpec(memory_space=pl.ANY),
                  pl.BlockSpec((None, 1, D, F), lambda b, tok, be, nv: (layer, be[b], 0, 0)),
                  pl.BlockSpec((None, 1, D, F), lambda b, tok, be, nv: (layer, be[b], 0, 0)),
                  pl.BlockSpec((None, 1, F, D), lambda b, tok, be, nv: (layer, be[b], 0, 0))],
        out_specs=pl.BlockSpec((DISPATCH_BLOCK, D // 2), lambda b, tok, be, nv: (b, 0)),
        scratch_shapes=[pltpu.VMEM((2, DISPATCH_BLOCK, D // 2), jnp.uint32), pltpu.SemaphoreType.DMA((2,))],
    )
    return pl.pallas_call(
        _expert_kernel,
        grid_spec=grid_spec,
        out_shape=jax.ShapeDtypeStruct((P, D // 2), jnp.uint32),
        compiler_params=_params("arbitrary"),
        name="moe_expert_ffn",
    )(slot_tok, blk_expert, n_valid, xp, w_gate, w_up, w_down)


COMBINE_TOKENS = 256


def _combine_kernel(dest_ref, ys_hbm, x_ref, gate_ref, g_ref, b_ref, of_ref, ob_ref, buf, sem,
                    *, n_tokens, tb, alpha):
    i = pl.program_id(0)

    def row_copy(half, k, r, src):
        return pltpu.make_async_copy(ys_hbm.at[pl.ds(src, 1)], buf.at[half, k, pl.ds(r, 1)], sem.at[half])

    def issue_rows(base, half, r0, count):
        for u in range(count):
            for k in range(TOP_K):
                row_copy(half, k, r0 + u, dest_ref[k * n_tokens + base + r0 + u]).start(priority=k % 2)

    def drain(half):
        def body(j, carry):
            for u in range(DMA_UNROLL * TOP_K):
                row_copy(half, 0, 0, 0).wait()
            return carry

        lax.fori_loop(0, tb // DMA_UNROLL, body, 0)

    @pl.when(i == 0)
    def _():
        lax.fori_loop(0, tb // DMA_UNROLL,
                      lambda j, c: (issue_rows(0, 0, j * DMA_UNROLL, DMA_UNROLL), c)[1], 0)

    half = i % 2
    drain(half)

    g = g_ref[...]
    b = b_ref[...]
    next_base = jnp.minimum(i + 1, pl.num_programs(0) - 1) * tb

    def body(r, carry):
        issue_rows(next_base, 1 - half, r * LN_ROWS, LN_ROWS)
        rows = pl.ds(pl.multiple_of(r * LN_ROWS, LN_ROWS), LN_ROWS)
        gates = gate_ref[rows, :]
        m = _unpack_rows(buf[half, 0, rows, :]) * gates[:, 0:1]
        for k in range(1, TOP_K):
            m = m + _unpack_rows(buf[half, k, rows, :]) * gates[:, k:k + 1]
        out = _layer_norm_rows(alpha * x_ref[rows, :] + m, g, b)
        of_ref[rows, :] = out
        ob_ref[rows, :] = out.astype(BF16)
        return carry

    lax.fori_loop(0, tb // LN_ROWS, body, 0)

    @pl.when(i == pl.num_programs(0) - 1)
    def _():
        drain(1 - half)


def _combine(ys, x, dest, gate_cols, g, b, alpha):
    L, D = x.shape
    tb = min(COMBINE_TOKENS, L)
    grid_spec = pltpu.PrefetchScalarGridSpec(
        num_scalar_prefetch=1,
        grid=(L // tb,),
        in_specs=[pl.BlockSpec(memory_space=pl.ANY),
                  pl.BlockSpec((tb, D), lambda i, d: (i, 0)),
                  pl.BlockSpec((tb, TOP_K), lambda i, d: (i, 0)),
                  pl.BlockSpec((1, D), lambda i, d: (0, 0)),
                  pl.BlockSpec((1, D), lambda i, d: (0, 0))],
        out_specs=[pl.BlockSpec((tb, D), lambda i, d: (i, 0)),
                   pl.BlockSpec((tb, D), lambda i, d: (i, 0))],
        scratch_shapes=[pltpu.VMEM((2, TOP_K, tb, D // 2), ys.dtype), pltpu.SemaphoreType.DMA((2,))],
    )
    return pl.pallas_call(
        functools.partial(_combine_kernel, n_tokens=L, tb=tb, alpha=alpha),
        grid_spec=grid_spec,
        out_shape=[jax.ShapeDtypeStruct((L, D), F32), jax.ShapeDtypeStruct((L, D), BF16)],
        compiler_params=_params("arbitrary"),
        name="moe_combine_layernorm",
    )(dest, ys, x, gate_cols, g.reshape(1, D), b.reshape(1, D))


def _moe_layer(x, xp, router_w, router_b, w_gate, w_up, w_down, layer, g, b, alpha):
    L, D = x.shape
    blk = DISPATCH_BLOCK
    expert, gate = _route(x, router_w, router_b)
    rank, counts = _expert_ranks(expert)
    padded = (counts + blk - 1) // blk * blk
    pends = jnp.cumsum(padded)
    pstarts = pends - padded
    nb = (L * TOP_K) // blk + N_EXPERTS
    n_valid = (pends[-1] // blk).astype(jnp.int32)
    dest = (jnp.sum(jnp.where(expert[None] == jnp.arange(N_EXPERTS)[:, None, None],
                              pstarts[:, None, None], 0), axis=0) + rank).astype(jnp.int32)
    blk_idx = jnp.minimum(jnp.arange(nb, dtype=jnp.int32), n_valid - 1)
    blk_expert = jnp.minimum(jnp.sum(pends[None, :] <= (blk_idx * blk)[:, None], axis=1),
                             N_EXPERTS - 1).astype(jnp.int32)
    dest_flat = dest.reshape(TOP_K * L)
    slot_tok = _slot_tokens(dest_flat, L, nb * blk)
    ys = _expert_ffn(xp, slot_tok, w_gate, w_up, w_down, layer, blk_expert, n_valid.reshape(1))
    return _combine(ys, x, dest_flat, jnp.transpose(gate), g, b, alpha)


def kernel(x, gla_w_in, gla_w_gate2, gla_b_gate, gla_norm_g, gla_w_out, sgu_w_in, sgu_ln_g, sgu_ln_b,
           sgu_w_s, sgu_b_s, sgu_w_out, router_w, router_b, moe_w_gate, moe_w_up, moe_w_down, ln_g, ln_b):
    B, L, D = x.shape
    depth = ln_g.shape[0]
    alpha = float((2 * depth) ** 0.25)
    gla_w_in, gla_w_out, sgu_w_in, sgu_w_out, moe_w_gate, moe_w_up, moe_w_down = (
        w.astype(BF16) for w in (gla_w_in, gla_w_out, sgu_w_in, sgu_w_out, moe_w_gate, moe_w_up, moe_w_down))
    outs = []
    for bi in range(B):
        xf = x[bi]
        xb = xf.astype(BF16)
        for i in range(depth):
            j = i // 2
            if i % 2 == 0:
                a = _gla_mixer(xb, gla_w_in, j, gla_w_gate2[j], gla_b_gate[j], gla_norm_g[j])
                w_out = gla_w_out
            else:
                zact = _matmul(xb, sgu_w_in, j, sgu_w_in.shape[2], BF16, act="gelu")
                a = _sgu_gate(zact, sgu_ln_g[j], sgu_ln_b[j], sgu_w_s[j], sgu_b_s[j])
                w_out = sgu_w_out
            xf, xp = _matmul_res_ln(a, w_out, j, xf, ln_g[i, 0], ln_b[i, 0], alpha)
            xf, xb = _moe_layer(xf, xp, router_w, router_b, moe_w_gate, moe_w_up, moe_w_down, i,
                                ln_g[i, 1], ln_b[i, 1], alpha)
        outs.append(xf)
    return jnp.stack(outs, axis=0)
```

```python
import functools

import jax
import jax.numpy as jnp
from jax import lax
from jax.experimental import pallas as pl
from jax.experimental.pallas import tpu as pltpu

GLA_HEADS = 8
GLA_GATE_RANK = 16
GLA_TAU = 16.0
GLA_CHUNK = 64
SGU_GROUPS = 8
SGU_CHUNK = 128
N_EXPERTS = 16
N_GROUPS = 4
EXPERTS_PER_GROUP = N_EXPERTS // N_GROUPS
TOP_K = 2
LN_EPS = 1e-5
RMS_EPS = 1e-6

LANES = 128
SUBLANES = 8
VMEM_LIMIT_BYTES = 56 * 1024 * 1024

DISPATCH_BLOCK = 256
NEG_BIG = -1e30

F32 = jnp.float32
BF16 = jnp.bfloat16
_NT = (((1,), (1,)), ((), ()))
_TN = (((0,), (0,)), ((), ()))


def _params(*sem):
    return pltpu.CompilerParams(dimension_semantics=sem, vmem_limit_bytes=VMEM_LIMIT_BYTES)


def _layer_norm_rows(y, g, b):
    mu = jnp.mean(y, axis=-1, keepdims=True)
    yc = y - mu
    var = jnp.mean(yc * yc, axis=-1, keepdims=True)
    return yc * lax.rsqrt(var + LN_EPS) * g + b


def _mm_kernel(x_ref, w_ref, o_ref, *, act):
    acc = jnp.dot(x_ref[...], w_ref[...], preferred_element_type=F32)
    if act == "gelu":
        acc = 0.5 * acc * (1.0 + lax.erf(acc * (2.0 ** -0.5)))
    o_ref[...] = acc.astype(o_ref.dtype)


def _tile(dim, target, align):
    t = min(target, dim) // align * align
    while dim % t:
        t -= align
    return t


def _matmul(x, w, layer, n, out_dtype, act=None, tm=1024, tn=1024):
    m, k = x.shape
    tm, tn = _tile(m, tm, SUBLANES), _tile(n, tn, LANES)
    return pl.pallas_call(
        functools.partial(_mm_kernel, act=act),
        grid=(m // tm, n // tn),
        in_specs=[pl.BlockSpec((tm, k), lambda i, j: (i, 0)),
                  pl.BlockSpec((None, k, tn), lambda i, j: (layer, 0, j))],
        out_specs=pl.BlockSpec((tm, tn), lambda i, j: (i, j)),
        out_shape=jax.ShapeDtypeStruct((m, n), out_dtype),
        compiler_params=_params("parallel", "arbitrary"),
        name="dense_matmul",
    )(x, w)


PACK_GROUP = 512
_HI_MASK = 0xFFFF0000


def _bf16_bits(x):
    return lax.bitcast_convert_type(x.astype(BF16).astype(F32), jnp.uint32)


def _pack_rows(x):
    half = PACK_GROUP // 2
    words = []
    for c0 in range(0, x.shape[1], PACK_GROUP):
        lo = _bf16_bits(x[:, c0:c0 + half]) >> 16
        hi = _bf16_bits(x[:, c0 + half:c0 + PACK_GROUP]) & jnp.uint32(_HI_MASK)
        words.append(hi | lo)
    return words[0] if len(words) == 1 else jnp.concatenate(words, axis=1)


def _unpack_rows(w):
    half = PACK_GROUP // 2
    cols = []
    for c0 in range(0, w.shape[1], half):
        piece = w[:, c0:c0 + half]
        cols.append(lax.bitcast_convert_type(piece << 16, F32))
        cols.append(lax.bitcast_convert_type(piece & jnp.uint32(_HI_MASK), F32))
    return jnp.concatenate(cols, axis=1)


LN_ROWS = 32


def _mm_ln_kernel(a_ref, w_ref, res_ref, g_ref, b_ref, of_ref, xp_ref, y_scr, mu_scr, rs_scr,
                  *, alpha, nj, tm, n):
    j = pl.program_id(1)

    @pl.when(j == 0)
    def _():
        mu_scr[...] = jnp.zeros_like(mu_scr)
        rs_scr[...] = jnp.zeros_like(rs_scr)

    @pl.when(j < nj)
    def _():
        y = alpha * res_ref[...] + jnp.dot(a_ref[...], w_ref[...], preferred_element_type=F32)
        y_scr[j] = y
        mu_scr[...] += jnp.sum(y, axis=-1, keepdims=True)
        rs_scr[...] += jnp.sum(y * y, axis=-1, keepdims=True)

    @pl.when(j == nj)
    def _():
        mu = mu_scr[...] * (1.0 / n)
        mu_scr[...] = mu
        rs_scr[...] = lax.rsqrt(rs_scr[...] * (1.0 / n) - mu * mu + LN_EPS)

    @pl.when(j >= nj)
    def _():
        t = j - nj
        g = g_ref[...]
        b = b_ref[...]

        def body(r, carry):
            rows = pl.ds(pl.multiple_of(r * LN_ROWS, LN_ROWS), LN_ROWS)
            out = (y_scr[t, rows, :] - mu_scr[rows, :]) * rs_scr[rows, :] * g + b
            of_ref[rows, :] = out
            xp_ref[rows, :] = _pack_rows(out)
            return carry

        lax.fori_loop(0, tm // LN_ROWS, body, 0, unroll=8)


def _matmul_res_ln(a, w, layer, res, g, b, alpha, tm=1024, tn=512):
    m, kdim = a.shape
    n = w.shape[2]
    tm, tn = _tile(m, tm, SUBLANES), _tile(n, tn, PACK_GROUP)
    nj = n // tn
    nb = m // tm
    fill_row = lambda i, j: jnp.where(j < nj, i, jnp.minimum(i + 1, nb - 1))
    fill_col = lambda j: jnp.where(j < nj, j, 0)
    emit = lambda j: jnp.maximum(j - nj, 0)
    return pl.pallas_call(
        functools.partial(_mm_ln_kernel, alpha=alpha, nj=nj, tm=tm, n=n),
        grid=(nb, 2 * nj),
        in_specs=[pl.BlockSpec((tm, kdim), lambda i, j: (fill_row(i, j), 0)),
                  pl.BlockSpec((None, kdim, tn), lambda i, j: (layer, 0, fill_col(j))),
                  pl.BlockSpec((tm, tn), lambda i, j: (fill_row(i, j), fill_col(j))),
                  pl.BlockSpec((1, tn), lambda i, j: (0, emit(j))),
                  pl.BlockSpec((1, tn), lambda i, j: (0, emit(j)))],
        out_specs=[pl.BlockSpec((tm, tn), lambda i, j: (i, emit(j))),
                   pl.BlockSpec((tm, tn // 2), lambda i, j: (i, emit(j)))],
        out_shape=[jax.ShapeDtypeStruct((m, n), F32),
                   jax.ShapeDtypeStruct((m, n // 2), jnp.uint32)],
        scratch_shapes=[pltpu.VMEM((nj, tm, tn), F32), pltpu.VMEM((tm, 1), F32), pltpu.VMEM((tm, 1), F32)],
        compiler_params=_params("parallel", "arbitrary"),
        name="matmul_residual_layernorm",
    )(a, w, res, g.reshape(1, n), b.reshape(1, n))


def _log_sigmoid(x):
    return jnp.minimum(x, 0.0) - jnp.log1p(jnp.exp(-jnp.abs(x)))


def _gla_kernel(*refs, reverse, finalize, n_chunks, scale):
    if finalize:
        (q_ref, k_ref, z_ref, qn_ref, kn_ref, zn_ref, v_ref, wg_ref, bg_ref, oprev_ref, r_ref, ng_ref,
         o_ref, g_scr, st_scr, oi_scr, u_scr, *prep) = refs
    else:
        (q_ref, k_ref, z_ref, qn_ref, kn_ref, zn_ref, v_ref, wg_ref, bg_ref,
         o_ref, g_scr, st_scr, oi_scr, u_scr, *prep) = refs
    set_a, set_b = prep[:4], prep[4:]
    C = GLA_CHUNK
    i = pl.program_id(1)

    row = lax.broadcasted_iota(jnp.int32, (C, C), 0)
    col = lax.broadcasted_iota(jnp.int32, (C, C), 1)
    if reverse:
        tri = jnp.where(col >= row, 1.0, 0.0).astype(BF16)
        mask = col > row
        last = 0
    else:
        tri = jnp.where(col <= row, 1.0, 0.0).astype(BF16)
        mask = col <= row
        last = C - 1

    chunk_rows = [slice(c * C, (c + 1) * C) for c in range(n_chunks)]

    def prepare_logits(zr):
        zh, zl = _split_bf16(zr[...])
        wh, wl = _split_bf16(wg_ref[...])
        return (jnp.dot(zh, wh, preferred_element_type=F32) + jnp.dot(zh, wl, preferred_element_type=F32)
                + jnp.dot(zl, wh, preferred_element_type=F32)) + bg_ref[...]

    def prepare_cumsums(logit):
        g_scr[...] = _log_sigmoid(logit) * (1.0 / GLA_TAU)
        sums = []
        for rows in chunk_rows:
            g_hi, g_lo = _split_bf16(g_scr[rows, :])
            sums.append(jnp.dot(tri, g_hi, preferred_element_type=F32)
                        + jnp.dot(tri, g_lo, preferred_element_type=F32))
        return sums

    def prepare_store(qr, kr, sums, dst):
        qd_scr, kd_scr, ks_scr, dec_scr = dst
        for c, rows in enumerate(chunk_rows):
            b = sums[c]
            b_last = b[last:last + 1, :]
            q = qr[rows, :].astype(F32) * scale
            k = kr[rows, :].astype(F32)
            qd_scr[rows, :] = (q * jnp.exp(b)).astype(BF16)
            kd_scr[rows, :] = (k * jnp.exp(-b)).astype(BF16)
            ks_scr[rows, :] = (k * jnp.exp(b_last - b)).astype(BF16)
            dec_scr[c:c + 1, :] = jnp.exp(b_last)

    def attend_scores(src):
        qd_scr, kd_scr, ks_scr, _ = src
        scores = [lax.dot_general(qd_scr[rows, :], kd_scr[rows, :], _NT, preferred_element_type=F32)
                  for rows in chunk_rows]
        for c, rows in enumerate(chunk_rows):
            u_scr[c] = lax.dot_general(v_ref[rows, :], ks_scr[rows, :], _TN, preferred_element_type=F32)
        return scores

    def attend_outputs(src, scores):
        qd_scr, _, _, dec_scr = src
        for c, rows in enumerate(chunk_rows):
            masked = jnp.where(mask, scores[c], 0.0).astype(BF16)
            oi_scr[rows, :] = jnp.dot(masked, v_ref[rows, :], preferred_element_type=F32)
        st = st_scr[...]
        for c in (reversed(range(n_chunks)) if reverse else range(n_chunks)):
            rows = chunk_rows[c]
            o = oi_scr[rows, :] + lax.dot_general(qd_scr[rows, :], st.astype(BF16), _NT,
                                                  preferred_element_type=F32)
            st = dec_scr[c:c + 1, :] * st + u_scr[c]
            if finalize:
                o = o + oprev_ref[rows, :].astype(F32)
                o = o * lax.rsqrt(jnp.mean(o * o, axis=-1, keepdims=True) + RMS_EPS)
                r = r_ref[rows, :].astype(F32)
                o = o * ng_ref[...] * (r * jax.nn.sigmoid(r))
            o_ref[rows, :] = o.astype(o_ref.dtype)
        st_scr[...] = st

    def step(cur, nxt):
        logit = prepare_logits(zn_ref)
        scores = attend_scores(cur)
        sums = prepare_cumsums(logit)
        attend_outputs(cur, scores)
        prepare_store(qn_ref, kn_ref, sums, nxt)

    @pl.when(i == 0)
    def _():
        st_scr[...] = jnp.zeros_like(st_scr)
        prepare_store(q_ref, k_ref, prepare_cumsums(prepare_logits(z_ref)), set_a)

    @pl.when(i % 2 == 0)
    def _():
        step(set_a, set_b)

    @pl.when(i % 2 == 1)
    def _():
        step(set_b, set_a)


def _gla_direction(proj, z, wg_pad, b_gate, *, reverse, o_prev=None, norm_g=None, tl=512):
    L = proj.shape[0]
    H = GLA_HEADS
    dk_total = wg_pad.shape[1]
    dk = dk_total // H
    dv_total = (proj.shape[1] - 2 * dk_total) // 2
    dv = dv_total // H
    tl = min(tl, L)
    nb = L // tl
    n_chunks = tl // GLA_CHUNK
    finalize = o_prev is not None
    blk = (lambda i: nb - 1 - i) if reverse else (lambda i: i)
    nxt = lambda i: blk(jnp.minimum(i + 1, nb - 1))
    k_off = dk_total // dk
    v_off = 2 * dk_total // dv
    r_off = v_off + dv_total // dv

    in_specs = [
        pl.BlockSpec((tl, dk), lambda h, i: (blk(i), h)),
        pl.BlockSpec((tl, dk), lambda h, i: (blk(i), k_off + h)),
        pl.BlockSpec((tl, LANES), lambda h, i: (blk(i), 0)),
        pl.BlockSpec((tl, dk), lambda h, i: (nxt(i), h)),
        pl.BlockSpec((tl, dk), lambda h, i: (nxt(i), k_off + h)),
        pl.BlockSpec((tl, LANES), lambda h, i: (nxt(i), 0)),
        pl.BlockSpec((tl, dv), lambda h, i: (blk(i), v_off + h)),
        pl.BlockSpec((LANES, dk), lambda h, i: (0, h)),
        pl.BlockSpec((1, dk), lambda h, i: (0, h)),
    ]
    args = [proj, proj, z, proj, proj, z, proj, wg_pad, b_gate.reshape(1, dk_total)]
    if finalize:
        in_specs += [
            pl.BlockSpec((tl, dv), lambda h, i: (blk(i), h)),
            pl.BlockSpec((tl, dv), lambda h, i: (blk(i), r_off + h)),
            pl.BlockSpec((1, dv), lambda h, i: (0, h)),
        ]
        args += [o_prev, proj, norm_g.reshape(1, dv_total)]
    prep_set = [pltpu.VMEM((tl, dk), BF16), pltpu.VMEM((tl, dk), BF16), pltpu.VMEM((tl, dk), BF16),
                pltpu.VMEM((n_chunks, dk), F32)]
    return pl.pallas_call(
        functools.partial(_gla_kernel, reverse=reverse, finalize=finalize,
                          n_chunks=n_chunks, scale=float(dk) ** -0.5),
        grid=(H, nb),
        in_specs=in_specs,
        out_specs=pl.BlockSpec((tl, dv), lambda h, i: (blk(i), h)),
        out_shape=jax.ShapeDtypeStruct((L, dv_total), BF16),
        scratch_shapes=[pltpu.VMEM((tl, dk), F32), pltpu.VMEM((dv, dk), F32),
                        pltpu.VMEM((tl, dv), F32), pltpu.VMEM((n_chunks, dv, dk), F32)] + prep_set + prep_set,
        compiler_params=_params("parallel", "arbitrary"),
        name="gla_backward_scan" if reverse else "gla_forward_scan",
    )(*args)


def _gla_mixer(xb, w_in, layer, w_gate2, b_gate, norm_g):
    dk_total = w_gate2.shape[-1]
    n_main = w_in.shape[2] - 2 * GLA_GATE_RANK
    rank = GLA_GATE_RANK
    proj = _matmul(xb, w_in, layer, n_main, BF16)
    w_z = jnp.zeros((1, w_in.shape[1], LANES), BF16).at[0, :, :2 * rank].set(w_in[layer, :, n_main:])
    z = _matmul(xb, w_z, 0, LANES, F32)
    wg_f = jnp.zeros((LANES, dk_total), F32).at[:rank].set(w_gate2[0])
    wg_b = jnp.zeros((LANES, dk_total), F32).at[rank:2 * rank].set(w_gate2[1])
    o_f = _gla_direction(proj, z, wg_f, b_gate[0], reverse=False)
    return _gla_direction(proj, z, wg_b, b_gate[1], reverse=True, o_prev=o_f, norm_g=norm_g)


def _sgu_kernel(u_ref, v_ref, g_ref, b_ref, ws_ref, bias_ref, o_ref, *, n_chunks, gw):
    C = SGU_CHUNK
    for c in range(n_chunks):
        rows = slice(c * C, (c + 1) * C)
        vn = _layer_norm_rows(v_ref[rows, :].astype(F32), g_ref[...], b_ref[...]).astype(BF16)
        for grp in range(SGU_GROUPS):
            cols = slice(grp * gw, (grp + 1) * gw)
            mixed = jnp.dot(ws_ref[grp], vn[:, cols], preferred_element_type=F32) + bias_ref[:, cols]
            o_ref[rows, cols] = (u_ref[rows, cols].astype(F32) * mixed).astype(BF16)


def _sgu_gate(zact, ln_g, ln_b, w_s, b_s, tl=256):
    L = zact.shape[0]
    half = zact.shape[1] // 2
    gw = half // SGU_GROUPS
    tl = min(tl, L)
    bias = jnp.repeat(jnp.transpose(b_s), gw, axis=1)
    return pl.pallas_call(
        functools.partial(_sgu_kernel, n_chunks=tl // SGU_CHUNK, gw=gw),
        grid=(L // tl,),
        in_specs=[pl.BlockSpec((tl, half), lambda i: (i, 0)),
                  pl.BlockSpec((tl, half), lambda i: (i, 1)),
                  pl.BlockSpec((1, half), lambda i: (0, 0)),
                  pl.BlockSpec((1, half), lambda i: (0, 0)),
                  pl.BlockSpec((SGU_GROUPS, SGU_CHUNK, SGU_CHUNK), lambda i: (0, 0, 0)),
                  pl.BlockSpec((SGU_CHUNK, half), lambda i: (0, 0))],
        out_specs=pl.BlockSpec((tl, half), lambda i: (i, 0)),
        out_shape=jax.ShapeDtypeStruct((L, half), BF16),
        compiler_params=_params("parallel"),
        name="sgu_spatial_gate",
    )(zact, zact, ln_g.reshape(1, half), ln_b.reshape(1, half), w_s.astype(BF16), bias)


ROUTER_ROWS = EXPERTS_PER_GROUP * SUBLANES


def _split_bf16(a):
    hi = a.astype(BF16)
    lo = (a - hi.astype(F32)).astype(BF16)
    return hi, lo


def _router_kernel(x_ref, wt_ref, rb_ref, o_ref):
    xh, xl = _split_bf16(x_ref[...])
    wh, wl = _split_bf16(wt_ref[...])
    logits = (lax.dot_general(wh, xh, _NT, preferred_element_type=F32)
              + lax.dot_general(wh, xl, _NT, preferred_element_type=F32)
              + lax.dot_general(wl, xh, _NT, preferred_element_type=F32))
    scores = jax.nn.sigmoid(logits)
    sel = scores + rb_ref[:, 0:1]
    T = logits.shape[1]
    s_m = [scores[SUBLANES * m:SUBLANES * (m + 1), :] for m in range(EXPERTS_PER_GROUP)]
    a_m = [sel[SUBLANES * m:SUBLANES * (m + 1), :] for m in range(EXPERTS_PER_GROUP)]
    hi1, lo1 = jnp.maximum(a_m[0], a_m[1]), jnp.minimum(a_m[0], a_m[1])
    hi2, lo2 = jnp.maximum(a_m[2], a_m[3]), jnp.minimum(a_m[2], a_m[3])
    group_score = jnp.maximum(hi1, hi2) + jnp.maximum(jnp.minimum(hi1, hi2), jnp.maximum(lo1, lo2))
    gid = lax.broadcasted_iota(jnp.int32, (SUBLANES, T), 0)
    gmax = jnp.max(group_score, axis=0, keepdims=True)
    g_star = jnp.min(jnp.where(group_score == gmax, gid, SUBLANES), axis=0, keepdims=True)
    onehot = gid == g_star
    v_m = [jnp.sum(jnp.where(onehot, a, 0.0), axis=0, keepdims=True) for a in a_m]
    r_m = [jnp.sum(jnp.where(onehot, s, 0.0), axis=0, keepdims=True) for s in s_m]
    best1, idx1, raw1 = v_m[0], jnp.zeros_like(g_star), r_m[0]
    for m in range(1, EXPERTS_PER_GROUP):
        take = v_m[m] > best1
        best1 = jnp.where(take, v_m[m], best1)
        idx1 = jnp.where(take, m, idx1)
        raw1 = jnp.where(take, r_m[m], raw1)
    best2 = jnp.full_like(best1, -jnp.inf)
    idx2, raw2 = jnp.zeros_like(g_star), jnp.zeros_like(raw1)
    for m in range(EXPERTS_PER_GROUP):
        take = jnp.logical_and(idx1 != m, v_m[m] > best2)
        best2 = jnp.where(take, v_m[m], best2)
        idx2 = jnp.where(take, m, idx2)
        raw2 = jnp.where(take, r_m[m], raw2)
    denom = raw1 + raw2
    e1 = (g_star * EXPERTS_PER_GROUP + idx1).astype(F32)
    e2 = (g_star * EXPERTS_PER_GROUP + idx2).astype(F32)
    rid = lax.broadcasted_iota(jnp.int32, (SUBLANES, T), 0)
    out = jnp.where(rid == 0, e1, jnp.where(rid == 1, e2, jnp.where(
        rid == 2, raw1 / denom, jnp.where(rid == 3, raw2 / denom, 0.0))))
    o_ref[...] = out


def _route(x, router_w, router_b, tl=512):
    L, D = x.shape
    tl = min(tl, L)
    w_t = jnp.zeros((EXPERTS_PER_GROUP, SUBLANES, D), F32)
    w_t = w_t.at[:, :N_GROUPS].set(router_w.T.reshape(N_GROUPS, EXPERTS_PER_GROUP, D).transpose(1, 0, 2))
    rb = jnp.full((EXPERTS_PER_GROUP, SUBLANES), NEG_BIG, F32)
    rb = rb.at[:, :N_GROUPS].set(router_b.astype(F32).reshape(N_GROUPS, EXPERTS_PER_GROUP).T)
    rb = jnp.broadcast_to(rb.reshape(ROUTER_ROWS, 1), (ROUTER_ROWS, LANES))
    out = pl.pallas_call(
        _router_kernel,
        grid=(L // tl,),
        in_specs=[pl.BlockSpec((tl, D), lambda i: (i, 0)),
                  pl.BlockSpec((ROUTER_ROWS, D), lambda i: (0, 0)),
                  pl.BlockSpec((ROUTER_ROWS, LANES), lambda i: (0, 0))],
        out_specs=pl.BlockSpec((SUBLANES, tl), lambda i: (0, i)),
        out_shape=jax.ShapeDtypeStruct((SUBLANES, L), F32),
        compiler_params=_params("parallel"),
        name="moe_router",
    )(x, w_t.reshape(ROUTER_ROWS, D), rb)
    expert = out[:TOP_K].astype(jnp.int32)
    gate = out[TOP_K:2 * TOP_K]
    return expert, gate


def _rank_kernel(e_ref, rank_ref, cnt_ref, carry_scr):
    @pl.when(jnp.logical_and(pl.program_id(0) == 0, pl.program_id(1) == 0))
    def _():
        carry_scr[...] = jnp.zeros_like(carry_scr)

    e = e_ref[0]
    T = e.shape[1]
    eid = lax.broadcasted_iota(jnp.int32, (N_EXPERTS, T), 0)
    onehot = eid == e
    oh = jnp.where(onehot, 1.0, 0.0)
    s_idx = lax.broadcasted_iota(jnp.int32, (T, T), 0)
    t_idx = lax.broadcasted_iota(jnp.int32, (T, T), 1)
    before = jnp.where(s_idx < t_idx, 1.0, 0.0).astype(BF16)
    prefix = jnp.dot(oh.astype(BF16), before, preferred_element_type=F32)
    carry = carry_scr[:, 0:1]
    rank_ref[0] = jnp.sum(jnp.where(onehot, prefix + carry, 0.0), axis=0, keepdims=True)
    carry_scr[...] = carry_scr[...] + jnp.sum(oh, axis=1, keepdims=True)
    cnt_ref[...] = carry_scr[...]


def _expert_ranks(expert, tl=512):
    K, L = expert.shape
    tl = min(tl, L)
    rank, cnt = pl.pallas_call(
        _rank_kernel,
        grid=(K, L // tl),
        in_specs=[pl.BlockSpec((1, 1, tl), lambda k, i: (k, 0, i))],
        out_specs=[pl.BlockSpec((1, 1, tl), lambda k, i: (k, 0, i)),
                   pl.BlockSpec((N_EXPERTS, LANES), lambda k, i: (0, 0))],
        out_shape=[jax.ShapeDtypeStruct((K, 1, L), F32),
                   jax.ShapeDtypeStruct((N_EXPERTS, LANES), F32)],
        scratch_shapes=[pltpu.VMEM((N_EXPERTS, LANES), F32)],
        compiler_params=_params("arbitrary", "arbitrary"),
        name="moe_expert_rank",
    )(expert.reshape(K, 1, L))
    return rank.reshape(K, L).astype(jnp.int32), cnt[:, 0].astype(jnp.int32)


DMA_UNROLL = 8
GATHER_DEPTH = 3


def _slot_token_kernel(dest_ref, tok_ref, *, n_tokens, n_slots):
    def clear(j, carry):
        for u in range(DMA_UNROLL):
            tok_ref[j * DMA_UNROLL + u] = 0
        return carry

    lax.fori_loop(0, n_slots // DMA_UNROLL, clear, 0)
    for k in range(TOP_K):
        def scatter(j, carry):
            for u in range(DMA_UNROLL):
                t = j * DMA_UNROLL + u
                tok_ref[dest_ref[k * n_tokens + t]] = t
            return carry

        lax.fori_loop(0, n_tokens // DMA_UNROLL, scatter, 0)


def _slot_tokens(dest, n_tokens, n_slots):
    return pl.pallas_call(
        functools.partial(_slot_token_kernel, n_tokens=n_tokens, n_slots=n_slots),
        in_specs=[pl.BlockSpec(memory_space=pltpu.SMEM)],
        out_specs=pl.BlockSpec(memory_space=pltpu.SMEM),
        out_shape=jax.ShapeDtypeStruct((n_slots,), jnp.int32),
        name="moe_slot_tokens",
    )(dest)


def _expert_kernel(tok_ref, bexp_ref, nvalid_ref, x_hbm, wg_ref, wu_ref, wd_ref, o_ref, xbuf, sem):
    b = pl.program_id(0)
    n_valid = nvalid_ref[0]
    rows = xbuf.shape[1]

    def row_copy(slot, r, tok):
        return pltpu.make_async_copy(x_hbm.at[pl.ds(tok, 1)], xbuf.at[slot, pl.ds(r, 1)], sem.at[slot])

    def drain(slot):
        def body(j, carry):
            for u in range(DMA_UNROLL):
                row_copy(slot, 0, 0).wait()
            return carry

        lax.fori_loop(0, rows // DMA_UNROLL, body, 0)

    last_blk = n_valid - 1

    @pl.when(b == 0)
    def _():
        for ahead in range(GATHER_DEPTH - 1):
            base = jnp.minimum(ahead, last_blk) * rows

            def issue(j, carry):
                for u in range(DMA_UNROLL):
                    r = j * DMA_UNROLL + u
                    row_copy(ahead, r, tok_ref[base + r]).start(priority=u % 2)
                return carry

            lax.fori_loop(0, rows // DMA_UNROLL, issue, 0)

    @pl.when(b < n_valid)
    def _():
        slot = b % GATHER_DEPTH
        drain(slot)
        ahead_slot = (b + GATHER_DEPTH - 1) % GATHER_DEPTH
        base = jnp.minimum(b + GATHER_DEPTH - 1, last_blk) * rows
        for r in range(rows):
            row_copy(ahead_slot, r, tok_ref[base + r]).start(priority=r % 2)
        x = _unpack_rows(xbuf[slot]).astype(BF16)
        hg = jnp.dot(x, wg_ref[0], preferred_element_type=F32)
        hu = jnp.dot(x, wu_ref[0], preferred_element_type=F32)
        a = (hg * jax.nn.sigmoid(hg) * hu).astype(BF16)
        o_ref[...] = _pack_rows(jnp.dot(a, wd_ref[0], preferred_element_type=F32))

    @pl.when(b == last_blk)
    def _():
        for ahead in range(1, GATHER_DEPTH):
            drain((b + ahead) % GATHER_DEPTH)

    @pl.when(b >= n_valid)
    def _():
        o_ref[...] = jnp.zeros_like(o_ref)


def _expert_ffn(xp, slot_tok, w_gate, w_up, w_down, layer, blk_expert, n_valid):
    D, F = w_gate.shape[2:]
    P = slot_tok.shape[0]
    nb = P // DISPATCH_BLOCK
    grid_spec = pltpu.PrefetchScalarGridSpec(
        num_scalar_prefetch=3,
        grid=(nb,),
        in_specs=[pl.BlockSpec(memory_space=pl.ANY),
                  pl.BlockSpec((None, 1, D, F), lambda b, tok, be, nv: (layer, be[b], 0, 0)),
                  pl.BlockSpec((None, 1, D, F), lambda b, tok, be, nv: (layer, be[b], 0, 0)),
                  pl.BlockSpec((None, 1, F, D), lambda b, tok, be, nv: (layer, be[b], 0, 0))],
        out_specs=pl.BlockSpec((DISPATCH_BLOCK, D // 2), lambda b, tok, be, nv: (b, 0)),
        scratch_shapes=[pltpu.VMEM((GATHER_DEPTH, DISPATCH_BLOCK, D // 2), jnp.uint32),
                        pltpu.SemaphoreType.DMA((GATHER_DEPTH,))],
    )
    return pl.pallas_call(
        _expert_kernel,
        grid_spec=grid_spec,
        out_shape=jax.ShapeDtypeStruct((P, D // 2), jnp.uint32),
        compiler_params=_params("arbitrary"),
        name="moe_expert_ffn",
    )(slot_tok, blk_expert, n_valid, xp, w_gate, w_up, w_down)


COMBINE_TOKENS = 256


def _combine_kernel(dest_ref, ys_hbm, x_ref, gate_ref, g_ref, b_ref, of_ref, ob_ref, buf, sem,
                    *, n_tokens, tb, alpha):
    i = pl.program_id(0)

    def row_copy(half, k, r, src):
        return pltpu.make_async_copy(ys_hbm.at[pl.ds(src, 1)], buf.at[half, k, pl.ds(r, 1)], sem.at[half])

    def issue_rows(base, half, r0, count):
        for u in range(count):
            for k in range(TOP_K):
                row_copy(half, k, r0 + u, dest_ref[k * n_tokens + base + r0 + u]).start(priority=k % 2)

    def drain(half):
        def body(j, carry):
            for u in range(DMA_UNROLL * TOP_K):
                row_copy(half, 0, 0, 0).wait()
            return carry

        lax.fori_loop(0, tb // DMA_UNROLL, body, 0)

    last_step = pl.num_programs(0) - 1

    @pl.when(i == 0)
    def _():
        for ahead in range(GATHER_DEPTH - 1):
            base = jnp.minimum(ahead, last_step) * tb
            lax.fori_loop(0, tb // DMA_UNROLL,
                          lambda j, c: (issue_rows(base, ahead, j * DMA_UNROLL, DMA_UNROLL), c)[1], 0)

    half = i % GATHER_DEPTH
    drain(half)

    g = g_ref[...]
    b = b_ref[...]
    ahead_half = (i + GATHER_DEPTH - 1) % GATHER_DEPTH
    ahead_base = jnp.minimum(i + GATHER_DEPTH - 1, last_step) * tb

    def body(r, carry):
        issue_rows(ahead_base, ahead_half, r * LN_ROWS, LN_ROWS)
        rows = pl.ds(pl.multiple_of(r * LN_ROWS, LN_ROWS), LN_ROWS)
        gates = gate_ref[rows, :]
        m = _unpack_rows(buf[half, 0, rows, :]) * gates[:, 0:1]
        for k in range(1, TOP_K):
            m = m + _unpack_rows(buf[half, k, rows, :]) * gates[:, k:k + 1]
        out = _layer_norm_rows(alpha * x_ref[rows, :] + m, g, b)
        of_ref[rows, :] = out
        ob_ref[rows, :] = out.astype(BF16)
        return carry

    lax.fori_loop(0, tb // LN_ROWS, body, 0)

    @pl.when(i == last_step)
    def _():
        for ahead in range(1, GATHER_DEPTH):
            drain((i + ahead) % GATHER_DEPTH)


def _combine(ys, x, dest, gate_cols, g, b, alpha):
    L, D = x.shape
    tb = min(COMBINE_TOKENS, L)
    grid_spec = pltpu.PrefetchScalarGridSpec(
        num_scalar_prefetch=1,
        grid=(L // tb,),
        in_specs=[pl.BlockSpec(memory_space=pl.ANY),
                  pl.BlockSpec((tb, D), lambda i, d: (i, 0)),
                  pl.BlockSpec((tb, TOP_K), lambda i, d: (i, 0)),
                  pl.BlockSpec((1, D), lambda i, d: (0, 0)),
                  pl.BlockSpec((1, D), lambda i, d: (0, 0))],
        out_specs=[pl.BlockSpec((tb, D), lambda i, d: (i, 0)),
                   pl.BlockSpec((tb, D), lambda i, d: (i, 0))],
        scratch_shapes=[pltpu.VMEM((GATHER_DEPTH, TOP_K, tb, D // 2), ys.dtype),
                        pltpu.SemaphoreType.DMA((GATHER_DEPTH,))],
    )
    return pl.pallas_call(
        functools.partial(_combine_kernel, n_tokens=L, tb=tb, alpha=alpha),
        grid_spec=grid_spec,
        out_shape=[jax.ShapeDtypeStruct((L, D), F32), jax.ShapeDtypeStruct((L, D), BF16)],
        compiler_params=_params("arbitrary"),
        name="moe_combine_layernorm",
    )(dest, ys, x, gate_cols, g.reshape(1, D), b.reshape(1, D))


def _moe_layer(x, xp, router_w, router_b, w_gate, w_up, w_down, layer, g, b, alpha):
    L, D = x.shape
    blk = DISPATCH_BLOCK
    expert, gate = _route(x, router_w, router_b)
    rank, counts = _expert_ranks(expert)
    padded = (counts + blk - 1) // blk * blk
    pends = jnp.cumsum(padded)
    pstarts = pends - padded
    nb = (L * TOP_K) // blk + N_EXPERTS
    n_valid = (pends[-1] // blk).astype(jnp.int32)
    dest = (jnp.sum(jnp.where(expert[None] == jnp.arange(N_EXPERTS)[:, None, None],
                              pstarts[:, None, None], 0), axis=0) + rank).astype(jnp.int32)
    blk_idx = jnp.minimum(jnp.arange(nb, dtype=jnp.int32), n_valid - 1)
    blk_expert = jnp.minimum(jnp.sum(pends[None, :] <= (blk_idx * blk)[:, None], axis=1),
                             N_EXPERTS - 1).astype(jnp.int32)
    dest_flat = dest.reshape(TOP_K * L)
    slot_tok = _slot_tokens(dest_flat, L, nb * blk)
    ys = _expert_ffn(xp, slot_tok, w_gate, w_up, w_down, layer, blk_expert, n_valid.reshape(1))
    return _combine(ys, x, dest_flat, jnp.transpose(gate), g, b, alpha)


def kernel(x, gla_w_in, gla_w_gate2, gla_b_gate, gla_norm_g, gla_w_out, sgu_w_in, sgu_ln_g, sgu_ln_b,
           sgu_w_s, sgu_b_s, sgu_w_out, router_w, router_b, moe_w_gate, moe_w_up, moe_w_down, ln_g, ln_b):
    B, L, D = x.shape
    depth = ln_g.shape[0]
    alpha = float((2 * depth) ** 0.25)
    gla_w_in, gla_w_out, sgu_w_in, sgu_w_out, moe_w_gate, moe_w_up, moe_w_down = (
        w.astype(BF16) for w in (gla_w_in, gla_w_out, sgu_w_in, sgu_w_out, moe_w_gate, moe_w_up, moe_w_down))
    outs = []
    for bi in range(B):
        xf = x[bi]
        xb = xf.astype(BF16)
        for i in range(depth):
            j = i // 2
            if i % 2 == 0:
                a = _gla_mixer(xb, gla_w_in, j, gla_w_gate2[j], gla_b_gate[j], gla_norm_g[j])
                w_out = gla_w_out
            else:
                zact = _matmul(xb, sgu_w_in, j, sgu_w_in.shape[2], BF16, act="gelu")
                a = _sgu_gate(zact, sgu_ln_g[j], sgu_ln_b[j], sgu_w_s[j], sgu_b_s[j])
                w_out = sgu_w_out
            xf, xp = _matmul_res_ln(a, w_out, j, xf, ln_g[i, 0], ln_b[i, 0], alpha)
            xf, xb = _moe_layer(xf, xp, router_w, router_b, moe_w_gate, moe_w_up, moe_w_down, i,
                                ln_g[i, 1], ln_b[i, 1], alpha)
        outs.append(xf)
    return jnp.stack(outs, axis=0)
```

```python
import functools

import jax
import jax.numpy as jnp
from jax import lax
from jax.experimental import pallas as pl
from jax.experimental.pallas import tpu as pltpu

GLA_HEADS = 8
GLA_GATE_RANK = 16
GLA_TAU = 16.0
GLA_CHUNK = 64
SGU_GROUPS = 8
SGU_CHUNK = 128
N_EXPERTS = 16
N_GROUPS = 4
EXPERTS_PER_GROUP = N_EXPERTS // N_GROUPS
TOP_K = 2
LN_EPS = 1e-5
RMS_EPS = 1e-6

LANES = 128
SUBLANES = 8
VMEM_LIMIT_BYTES = 56 * 1024 * 1024

DISPATCH_BLOCK = 256
NEG_BIG = -1e30

F32 = jnp.float32
BF16 = jnp.bfloat16
_NT = (((1,), (1,)), ((), ()))
_TN = (((0,), (0,)), ((), ()))


def _params(*sem):
    return pltpu.CompilerParams(dimension_semantics=sem, vmem_limit_bytes=VMEM_LIMIT_BYTES)


def _layer_norm_rows(y, g, b):
    mu = jnp.mean(y, axis=-1, keepdims=True)
    yc = y - mu
    var = jnp.mean(yc * yc, axis=-1, keepdims=True)
    return yc * lax.rsqrt(var + LN_EPS) * g + b


def _mm_kernel(x_ref, w_ref, o_ref, *, act):
    acc = jnp.dot(x_ref[...], w_ref[...], preferred_element_type=F32)
    if act == "gelu":
        acc = 0.5 * acc * (1.0 + lax.erf(acc * (2.0 ** -0.5)))
    o_ref[...] = acc.astype(o_ref.dtype)


def _tile(dim, target, align):
    t = min(target, dim) // align * align
    while dim % t:
        t -= align
    return t


def _matmul(x, w, layer, n, out_dtype, act=None, tm=1024, tn=1024):
    m, k = x.shape
    tm, tn = _tile(m, tm, SUBLANES), _tile(n, tn, LANES)
    return pl.pallas_call(
        functools.partial(_mm_kernel, act=act),
        grid=(m // tm, n // tn),
        in_specs=[pl.BlockSpec((tm, k), lambda i, j: (i, 0)),
                  pl.BlockSpec((None, k, tn), lambda i, j: (layer, 0, j))],
        out_specs=pl.BlockSpec((tm, tn), lambda i, j: (i, j)),
        out_shape=jax.ShapeDtypeStruct((m, n), out_dtype),
        compiler_params=_params("parallel", "arbitrary"),
        name="dense_matmul",
    )(x, w)


PACK_GROUP = 512
_HI_MASK = 0xFFFF0000


def _bf16_bits(x):
    return lax.bitcast_convert_type(x.astype(BF16).astype(F32), jnp.uint32)


def _pack_rows(x):
    half = PACK_GROUP // 2
    words = []
    for c0 in range(0, x.shape[1], PACK_GROUP):
        lo = _bf16_bits(x[:, c0:c0 + half]) >> 16
        hi = _bf16_bits(x[:, c0 + half:c0 + PACK_GROUP]) & jnp.uint32(_HI_MASK)
        words.append(hi | lo)
    return words[0] if len(words) == 1 else jnp.concatenate(words, axis=1)


def _unpack_rows(w):
    half = PACK_GROUP // 2
    cols = []
    for c0 in range(0, w.shape[1], half):
        piece = w[:, c0:c0 + half]
        cols.append(lax.bitcast_convert_type(piece << 16, F32))
        cols.append(lax.bitcast_convert_type(piece & jnp.uint32(_HI_MASK), F32))
    return jnp.concatenate(cols, axis=1)


LN_ROWS = 32


def _mm_ln_kernel(a_ref, w_ref, res_ref, g_ref, b_ref, of_ref, xp_ref, y_scr, mu_scr, rs_scr,
                  *, alpha, nj, tm, n):
    j = pl.program_id(1)

    @pl.when(j == 0)
    def _():
        mu_scr[...] = jnp.zeros_like(mu_scr)
        rs_scr[...] = jnp.zeros_like(rs_scr)

    @pl.when(j < nj)
    def _():
        y = alpha * res_ref[...] + jnp.dot(a_ref[...], w_ref[...], preferred_element_type=F32)
        y_scr[j] = y
        mu_scr[...] += jnp.sum(y, axis=-1, keepdims=True)
        rs_scr[...] += jnp.sum(y * y, axis=-1, keepdims=True)

    @pl.when(j == nj)
    def _():
        mu = mu_scr[...] * (1.0 / n)
        mu_scr[...] = mu
        rs_scr[...] = lax.rsqrt(rs_scr[...] * (1.0 / n) - mu * mu + LN_EPS)

    @pl.when(j >= nj)
    def _():
        t = j - nj
        g = g_ref[...]
        b = b_ref[...]

        def body(r, carry):
            rows = pl.ds(pl.multiple_of(r * LN_ROWS, LN_ROWS), LN_ROWS)
            out = (y_scr[t, rows, :] - mu_scr[rows, :]) * rs_scr[rows, :] * g + b
            of_ref[rows, :] = out
            xp_ref[rows, :] = _pack_rows(out)
            return carry

        lax.fori_loop(0, tm // LN_ROWS, body, 0, unroll=8)


def _matmul_res_ln(a, w, layer, res, g, b, alpha, tm=1024, tn=512):
    m, kdim = a.shape
    n = w.shape[2]
    tm, tn = _tile(m, tm, SUBLANES), _tile(n, tn, PACK_GROUP)
    nj = n // tn
    nb = m // tm
    fill_row = lambda i, j: jnp.where(j < nj, i, jnp.minimum(i + 1, nb - 1))
    fill_col = lambda j: jnp.where(j < nj, j, 0)
    emit = lambda j: jnp.maximum(j - nj, 0)
    return pl.pallas_call(
        functools.partial(_mm_ln_kernel, alpha=alpha, nj=nj, tm=tm, n=n),
        grid=(nb, 2 * nj),
        in_specs=[pl.BlockSpec((tm, kdim), lambda i, j: (fill_row(i, j), 0)),
                  pl.BlockSpec((None, kdim, tn), lambda i, j: (layer, 0, fill_col(j))),
                  pl.BlockSpec((tm, tn), lambda i, j: (fill_row(i, j), fill_col(j))),
                  pl.BlockSpec((1, tn), lambda i, j: (0, emit(j))),
                  pl.BlockSpec((1, tn), lambda i, j: (0, emit(j)))],
        out_specs=[pl.BlockSpec((tm, tn), lambda i, j: (i, emit(j))),
                   pl.BlockSpec((tm, tn // 2), lambda i, j: (i, emit(j)))],
        out_shape=[jax.ShapeDtypeStruct((m, n), F32),
                   jax.ShapeDtypeStruct((m, n // 2), jnp.uint32)],
        scratch_shapes=[pltpu.VMEM((nj, tm, tn), F32), pltpu.VMEM((tm, 1), F32), pltpu.VMEM((tm, 1), F32)],
        compiler_params=_params("parallel", "arbitrary"),
        name="matmul_residual_layernorm",
    )(a, w, res, g.reshape(1, n), b.reshape(1, n))


def _log_sigmoid(x):
    return jnp.minimum(x, 0.0) - jnp.log1p(jnp.exp(-jnp.abs(x)))


def _gla_kernel(*refs, reverse, finalize, n_chunks, scale):
    if finalize:
        (q_ref, k_ref, z_ref, qn_ref, kn_ref, zn_ref, v_ref, wg_ref, bg_ref, oprev_ref, r_ref, ng_ref,
         o_ref, g_scr, st_scr, oi_scr, u_scr, *prep) = refs
    else:
        (q_ref, k_ref, z_ref, qn_ref, kn_ref, zn_ref, v_ref, wg_ref, bg_ref,
         o_ref, g_scr, st_scr, oi_scr, u_scr, *prep) = refs
    set_a, set_b = prep[:4], prep[4:]
    C = GLA_CHUNK
    i = pl.program_id(1)

    row = lax.broadcasted_iota(jnp.int32, (C, C), 0)
    col = lax.broadcasted_iota(jnp.int32, (C, C), 1)
    if reverse:
        tri = jnp.where(col >= row, 1.0, 0.0).astype(BF16)
        mask = col > row
        last = 0
    else:
        tri = jnp.where(col <= row, 1.0, 0.0).astype(BF16)
        mask = col <= row
        last = C - 1

    chunk_rows = [slice(c * C, (c + 1) * C) for c in range(n_chunks)]

    def prepare_logits(zr):
        zh, zl = _split_bf16(zr[...])
        wh, wl = _split_bf16(wg_ref[...])
        return (jnp.dot(zh, wh, preferred_element_type=F32) + jnp.dot(zh, wl, preferred_element_type=F32)
                + jnp.dot(zl, wh, preferred_element_type=F32)) + bg_ref[...]

    def prepare_cumsums(logit):
        g_scr[...] = _log_sigmoid(logit) * (1.0 / GLA_TAU)
        sums = []
        for rows in chunk_rows:
            g_hi, g_lo = _split_bf16(g_scr[rows, :])
            sums.append(jnp.dot(tri, g_hi, preferred_element_type=F32)
                        + jnp.dot(tri, g_lo, preferred_element_type=F32))
        return sums

    def prepare_store(qr, kr, sums, dst):
        qd_scr, kd_scr, ks_scr, dec_scr = dst
        for c, rows in enumerate(chunk_rows):
            b = sums[c]
            b_last = b[last:last + 1, :]
            q = qr[rows, :].astype(F32) * scale
            k = kr[rows, :].astype(F32)
            qd_scr[rows, :] = (q * jnp.exp(b)).astype(BF16)
            kd_scr[rows, :] = (k * jnp.exp(-b)).astype(BF16)
            ks_scr[rows, :] = (k * jnp.exp(b_last - b)).astype(BF16)
            dec_scr[c:c + 1, :] = jnp.exp(b_last)

    def attend_scores(src):
        qd_scr, kd_scr, ks_scr, _ = src
        scores = [lax.dot_general(qd_scr[rows, :], kd_scr[rows, :], _NT, preferred_element_type=F32)
                  for rows in chunk_rows]
        for c, rows in enumerate(chunk_rows):
            u_scr[c] = lax.dot_general(v_ref[rows, :], ks_scr[rows, :], _TN, preferred_element_type=F32)
        return scores

    def attend_outputs(src, scores):
        qd_scr, _, _, dec_scr = src
        for c, rows in enumerate(chunk_rows):
            masked = jnp.where(mask, scores[c], 0.0).astype(BF16)
            oi_scr[rows, :] = jnp.dot(masked, v_ref[rows, :], preferred_element_type=F32)
        st = st_scr[...]
        for c in (reversed(range(n_chunks)) if reverse else range(n_chunks)):
            rows = chunk_rows[c]
            o = oi_scr[rows, :] + lax.dot_general(qd_scr[rows, :], st.astype(BF16), _NT,
                                                  preferred_element_type=F32)
            st = dec_scr[c:c + 1, :] * st + u_scr[c]
            if finalize:
                o = o + oprev_ref[rows, :].astype(F32)
                o = o * lax.rsqrt(jnp.mean(o * o, axis=-1, keepdims=True) + RMS_EPS)
                r = r_ref[rows, :].astype(F32)
                o = o * ng_ref[...] * (r * jax.nn.sigmoid(r))
            o_ref[rows, :] = o.astype(o_ref.dtype)
        st_scr[...] = st

    def step(cur, nxt):
        logit = prepare_logits(zn_ref)
        scores = attend_scores(cur)
        sums = prepare_cumsums(logit)
        attend_outputs(cur, scores)
        prepare_store(qn_ref, kn_ref, sums, nxt)

    @pl.when(i == 0)
    def _():
        st_scr[...] = jnp.zeros_like(st_scr)
        prepare_store(q_ref, k_ref, prepare_cumsums(prepare_logits(z_ref)), set_a)

    @pl.when(i % 2 == 0)
    def _():
        step(set_a, set_b)

    @pl.when(i % 2 == 1)
    def _():
        step(set_b, set_a)


def _gla_direction(proj, z, wg_pad, b_gate, *, reverse, o_prev=None, norm_g=None, tl=1024):
    L = proj.shape[0]
    H = GLA_HEADS
    dk_total = wg_pad.shape[1]
    dk = dk_total // H
    dv_total = (proj.shape[1] - 2 * dk_total) // 2
    dv = dv_total // H
    tl = min(tl, L)
    nb = L // tl
    n_chunks = tl // GLA_CHUNK
    finalize = o_prev is not None
    blk = (lambda i: nb - 1 - i) if reverse else (lambda i: i)
    nxt = lambda i: blk(jnp.minimum(i + 1, nb - 1))
    k_off = dk_total // dk
    v_off = 2 * dk_total // dv
    r_off = v_off + dv_total // dv

    in_specs = [
        pl.BlockSpec((tl, dk), lambda h, i: (blk(i), h)),
        pl.BlockSpec((tl, dk), lambda h, i: (blk(i), k_off + h)),
        pl.BlockSpec((tl, LANES), lambda h, i: (blk(i), 0)),
        pl.BlockSpec((tl, dk), lambda h, i: (nxt(i), h)),
        pl.BlockSpec((tl, dk), lambda h, i: (nxt(i), k_off + h)),
        pl.BlockSpec((tl, LANES), lambda h, i: (nxt(i), 0)),
        pl.BlockSpec((tl, dv), lambda h, i: (blk(i), v_off + h)),
        pl.BlockSpec((LANES, dk), lambda h, i: (0, h)),
        pl.BlockSpec((1, dk), lambda h, i: (0, h)),
    ]
    args = [proj, proj, z, proj, proj, z, proj, wg_pad, b_gate.reshape(1, dk_total)]
    if finalize:
        in_specs += [
            pl.BlockSpec((tl, dv), lambda h, i: (blk(i), h)),
            pl.BlockSpec((tl, dv), lambda h, i: (blk(i), r_off + h)),
            pl.BlockSpec((1, dv), lambda h, i: (0, h)),
        ]
        args += [o_prev, proj, norm_g.reshape(1, dv_total)]
    prep_set = [pltpu.VMEM((tl, dk), BF16), pltpu.VMEM((tl, dk), BF16), pltpu.VMEM((tl, dk), BF16),
                pltpu.VMEM((n_chunks, dk), F32)]
    return pl.pallas_call(
        functools.partial(_gla_kernel, reverse=reverse, finalize=finalize,
                          n_chunks=n_chunks, scale=float(dk) ** -0.5),
        grid=(H, nb),
        in_specs=in_specs,
        out_specs=pl.BlockSpec((tl, dv), lambda h, i: (blk(i), h)),
        out_shape=jax.ShapeDtypeStruct((L, dv_total), BF16),
        scratch_shapes=[pltpu.VMEM((tl, dk), F32), pltpu.VMEM((dv, dk), F32),
                        pltpu.VMEM((tl, dv), F32), pltpu.VMEM((n_chunks, dv, dk), F32)] + prep_set + prep_set,
        compiler_params=_params("parallel", "arbitrary"),
        name="gla_backward_scan" if reverse else "gla_forward_scan",
    )(*args)


def _gla_mixer(xb, w_main, w_z, layer, w_gate2, b_gate, norm_g):
    dk_total = w_gate2.shape[-1]
    rank = GLA_GATE_RANK
    proj = _matmul(xb, w_main, layer, w_main.shape[2], BF16)
    z = _matmul(xb, w_z, layer, LANES, F32)
    wg_f = jnp.zeros((LANES, dk_total), F32).at[:rank].set(w_gate2[0])
    wg_b = jnp.zeros((LANES, dk_total), F32).at[rank:2 * rank].set(w_gate2[1])
    o_f = _gla_direction(proj, z, wg_f, b_gate[0], reverse=False)
    return _gla_direction(proj, z, wg_b, b_gate[1], reverse=True, o_prev=o_f, norm_g=norm_g)


def _sgu_kernel(u_ref, v_ref, g_ref, b_ref, ws_ref, bias_ref, o_ref, *, n_chunks, gw):
    C = SGU_CHUNK
    for c in range(n_chunks):
        rows = slice(c * C, (c + 1) * C)
        vn = _layer_norm_rows(v_ref[rows, :].astype(F32), g_ref[...], b_ref[...]).astype(BF16)
        for grp in range(SGU_GROUPS):
            cols = slice(grp * gw, (grp + 1) * gw)
            mixed = jnp.dot(ws_ref[grp], vn[:, cols], preferred_element_type=F32) + bias_ref[:, cols]
            o_ref[rows, cols] = (u_ref[rows, cols].astype(F32) * mixed).astype(BF16)


def _sgu_gate(zact, ln_g, ln_b, w_s, b_s, tl=256):
    L = zact.shape[0]
    half = zact.shape[1] // 2
    gw = half // SGU_GROUPS
    tl = min(tl, L)
    bias = jnp.repeat(jnp.transpose(b_s), gw, axis=1)
    return pl.pallas_call(
        functools.partial(_sgu_kernel, n_chunks=tl // SGU_CHUNK, gw=gw),
        grid=(L // tl,),
        in_specs=[pl.BlockSpec((tl, half), lambda i: (i, 0)),
                  pl.BlockSpec((tl, half), lambda i: (i, 1)),
                  pl.BlockSpec((1, half), lambda i: (0, 0)),
                  pl.BlockSpec((1, half), lambda i: (0, 0)),
                  pl.BlockSpec((SGU_GROUPS, SGU_CHUNK, SGU_CHUNK), lambda i: (0, 0, 0)),
                  pl.BlockSpec((SGU_CHUNK, half), lambda i: (0, 0))],
        out_specs=pl.BlockSpec((tl, half), lambda i: (i, 0)),
        out_shape=jax.ShapeDtypeStruct((L, half), BF16),
        compiler_params=_params("parallel"),
        name="sgu_spatial_gate",
    )(zact, zact, ln_g.reshape(1, half), ln_b.reshape(1, half), w_s.astype(BF16), bias)


ROUTER_ROWS = EXPERTS_PER_GROUP * SUBLANES


def _split_bf16(a):
    hi = a.astype(BF16)
    lo = (a - hi.astype(F32)).astype(BF16)
    return hi, lo


def _router_kernel(x_ref, wt_ref, rb_ref, o_ref):
    xh, xl = _split_bf16(x_ref[...])
    wh, wl = _split_bf16(wt_ref[...])
    logits = (lax.dot_general(wh, xh, _NT, preferred_element_type=F32)
              + lax.dot_general(wh, xl, _NT, preferred_element_type=F32)
              + lax.dot_general(wl, xh, _NT, preferred_element_type=F32))
    scores = jax.nn.sigmoid(logits)
    sel = scores + rb_ref[:, 0:1]
    T = logits.shape[1]
    s_m = [scores[SUBLANES * m:SUBLANES * (m + 1), :] for m in range(EXPERTS_PER_GROUP)]
    a_m = [sel[SUBLANES * m:SUBLANES * (m + 1), :] for m in range(EXPERTS_PER_GROUP)]
    hi1, lo1 = jnp.maximum(a_m[0], a_m[1]), jnp.minimum(a_m[0], a_m[1])
    hi2, lo2 = jnp.maximum(a_m[2], a_m[3]), jnp.minimum(a_m[2], a_m[3])
    group_score = jnp.maximum(hi1, hi2) + jnp.maximum(jnp.minimum(hi1, hi2), jnp.maximum(lo1, lo2))
    gid = lax.broadcasted_iota(jnp.int32, (SUBLANES, T), 0)
    gmax = jnp.max(group_score, axis=0, keepdims=True)
    g_star = jnp.min(jnp.where(group_score == gmax, gid, SUBLANES), axis=0, keepdims=True)
    onehot = gid == g_star
    v_m = [jnp.sum(jnp.where(onehot, a, 0.0), axis=0, keepdims=True) for a in a_m]
    r_m = [jnp.sum(jnp.where(onehot, s, 0.0), axis=0, keepdims=True) for s in s_m]
    best1, idx1, raw1 = v_m[0], jnp.zeros_like(g_star), r_m[0]
    for m in range(1, EXPERTS_PER_GROUP):
        take = v_m[m] > best1
        best1 = jnp.where(take, v_m[m], best1)
        idx1 = jnp.where(take, m, idx1)
        raw1 = jnp.where(take, r_m[m], raw1)
    best2 = jnp.full_like(best1, -jnp.inf)
    idx2, raw2 = jnp.zeros_like(g_star), jnp.zeros_like(raw1)
    for m in range(EXPERTS_PER_GROUP):
        take = jnp.logical_and(idx1 != m, v_m[m] > best2)
        best2 = jnp.where(take, v_m[m], best2)
        idx2 = jnp.where(take, m, idx2)
        raw2 = jnp.where(take, r_m[m], raw2)
    denom = raw1 + raw2
    e1 = (g_star * EXPERTS_PER_GROUP + idx1).astype(F32)
    e2 = (g_star * EXPERTS_PER_GROUP + idx2).astype(F32)
    rid = lax.broadcasted_iota(jnp.int32, (SUBLANES, T), 0)
    out = jnp.where(rid == 0, e1, jnp.where(rid == 1, e2, jnp.where(
        rid == 2, raw1 / denom, jnp.where(rid == 3, raw2 / denom, 0.0))))
    o_ref[...] = out


def _route(x, router_w, router_b, tl=512):
    L, D = x.shape
    tl = min(tl, L)
    w_t = jnp.zeros((EXPERTS_PER_GROUP, SUBLANES, D), F32)
    w_t = w_t.at[:, :N_GROUPS].set(router_w.T.reshape(N_GROUPS, EXPERTS_PER_GROUP, D).transpose(1, 0, 2))
    rb = jnp.full((EXPERTS_PER_GROUP, SUBLANES), NEG_BIG, F32)
    rb = rb.at[:, :N_GROUPS].set(router_b.astype(F32).reshape(N_GROUPS, EXPERTS_PER_GROUP).T)
    rb = jnp.broadcast_to(rb.reshape(ROUTER_ROWS, 1), (ROUTER_ROWS, LANES))
    out = pl.pallas_call(
        _router_kernel,
        grid=(L // tl,),
        in_specs=[pl.BlockSpec((tl, D), lambda i: (i, 0)),
                  pl.BlockSpec((ROUTER_ROWS, D), lambda i: (0, 0)),
                  pl.BlockSpec((ROUTER_ROWS, LANES), lambda i: (0, 0))],
        out_specs=pl.BlockSpec((SUBLANES, tl), lambda i: (0, i)),
        out_shape=jax.ShapeDtypeStruct((SUBLANES, L), F32),
        compiler_params=_params("parallel"),
        name="moe_router",
    )(x, w_t.reshape(ROUTER_ROWS, D), rb)
    expert = out[:TOP_K].astype(jnp.int32)
    gate = out[TOP_K:2 * TOP_K]
    return expert, gate


def _rank_kernel(e_ref, rank_ref, cnt_ref, carry_scr):
    @pl.when(jnp.logical_and(pl.program_id(0) == 0, pl.program_id(1) == 0))
    def _():
        carry_scr[...] = jnp.zeros_like(carry_scr)

    e = e_ref[0]
    T = e.shape[1]
    eid = lax.broadcasted_iota(jnp.int32, (N_EXPERTS, T), 0)
    onehot = eid == e
    oh = jnp.where(onehot, 1.0, 0.0)
    s_idx = lax.broadcasted_iota(jnp.int32, (T, T), 0)
    t_idx = lax.broadcasted_iota(jnp.int32, (T, T), 1)
    before = jnp.where(s_idx < t_idx, 1.0, 0.0).astype(BF16)
    prefix = jnp.dot(oh.astype(BF16), before, preferred_element_type=F32)
    carry = carry_scr[:, 0:1]
    rank_ref[0] = jnp.sum(jnp.where(onehot, prefix + carry, 0.0), axis=0, keepdims=True)
    carry_scr[...] = carry_scr[...] + jnp.sum(oh, axis=1, keepdims=True)
    cnt_ref[...] = carry_scr[...]


def _expert_ranks(expert, tl=512):
    K, L = expert.shape
    tl = min(tl, L)
    rank, cnt = pl.pallas_call(
        _rank_kernel,
        grid=(K, L // tl),
        in_specs=[pl.BlockSpec((1, 1, tl), lambda k, i: (k, 0, i))],
        out_specs=[pl.BlockSpec((1, 1, tl), lambda k, i: (k, 0, i)),
                   pl.BlockSpec((N_EXPERTS, LANES), lambda k, i: (0, 0))],
        out_shape=[jax.ShapeDtypeStruct((K, 1, L), F32),
                   jax.ShapeDtypeStruct((N_EXPERTS, LANES), F32)],
        scratch_shapes=[pltpu.VMEM((N_EXPERTS, LANES), F32)],
        compiler_params=_params("arbitrary", "arbitrary"),
        name="moe_expert_rank",
    )(expert.reshape(K, 1, L))
    return rank.reshape(K, L).astype(jnp.int32), cnt[:, 0].astype(jnp.int32)


DMA_UNROLL = 8
GATHER_DEPTH = 3


def _slot_token_kernel(dest_ref, tok_ref, *, n_tokens, n_slots):
    def clear(j, carry):
        for u in range(DMA_UNROLL):
            tok_ref[j * DMA_UNROLL + u] = 0
        return carry

    lax.fori_loop(0, n_slots // DMA_UNROLL, clear, 0)
    for k in range(TOP_K):
        def scatter(j, carry):
            for u in range(DMA_UNROLL):
                t = j * DMA_UNROLL + u
                tok_ref[dest_ref[k * n_tokens + t]] = t
            return carry

        lax.fori_loop(0, n_tokens // DMA_UNROLL, scatter, 0)


def _slot_tokens(dest, n_tokens, n_slots):
    return pl.pallas_call(
        functools.partial(_slot_token_kernel, n_tokens=n_tokens, n_slots=n_slots),
        in_specs=[pl.BlockSpec(memory_space=pltpu.SMEM)],
        out_specs=pl.BlockSpec(memory_space=pltpu.SMEM),
        out_shape=jax.ShapeDtypeStruct((n_slots,), jnp.int32),
        name="moe_slot_tokens",
    )(dest)


def _expert_kernel(tok_ref, bexp_ref, nvalid_ref, x_hbm, wg_ref, wu_ref, wd_ref, o_ref, xbuf, sem):
    b = pl.program_id(0)
    n_valid = nvalid_ref[0]
    rows = xbuf.shape[1]

    def row_copy(slot, r, tok):
        return pltpu.make_async_copy(x_hbm.at[pl.ds(tok, 1)], xbuf.at[slot, pl.ds(r, 1)], sem.at[slot])

    def drain(slot):
        def body(j, carry):
            for u in range(DMA_UNROLL):
                row_copy(slot, 0, 0).wait()
            return carry

        lax.fori_loop(0, rows // DMA_UNROLL, body, 0)

    last_blk = n_valid - 1

    @pl.when(b == 0)
    def _():
        for ahead in range(GATHER_DEPTH - 1):
            base = jnp.minimum(ahead, last_blk) * rows

            def issue(j, carry):
                for u in range(DMA_UNROLL):
                    r = j * DMA_UNROLL + u
                    row_copy(ahead, r, tok_ref[base + r]).start(priority=u % 2)
                return carry

            lax.fori_loop(0, rows // DMA_UNROLL, issue, 0)

    @pl.when(b < n_valid)
    def _():
        slot = b % GATHER_DEPTH
        drain(slot)
        ahead_slot = (b + GATHER_DEPTH - 1) % GATHER_DEPTH
        base = jnp.minimum(b + GATHER_DEPTH - 1, last_blk) * rows
        for r in range(rows):
            row_copy(ahead_slot, r, tok_ref[base + r]).start(priority=r % 2)
        x = _unpack_rows(xbuf[slot]).astype(BF16)
        hg = jnp.dot(x, wg_ref[0], preferred_element_type=F32)
        hu = jnp.dot(x, wu_ref[0], preferred_element_type=F32)
        a = (hg * jax.nn.sigmoid(hg) * hu).astype(BF16)
        o_ref[...] = _pack_rows(jnp.dot(a, wd_ref[0], preferred_element_type=F32))

    @pl.when(b == last_blk)
    def _():
        for ahead in range(1, GATHER_DEPTH):
            drain((b + ahead) % GATHER_DEPTH)

    @pl.when(b >= n_valid)
    def _():
        o_ref[...] = jnp.zeros_like(o_ref)


def _expert_ffn(xp, slot_tok, w_gate, w_up, w_down, layer, blk_expert, n_valid):
    D, F = w_gate.shape[2:]
    P = slot_tok.shape[0]
    nb = P // DISPATCH_BLOCK
    grid_spec = pltpu.PrefetchScalarGridSpec(
        num_scalar_prefetch=3,
        grid=(nb,),
        in_specs=[pl.BlockSpec(memory_space=pl.ANY),
                  pl.BlockSpec((None, 1, D, F), lambda b, tok, be, nv: (layer, be[b], 0, 0)),
                  pl.BlockSpec((None, 1, D, F), lambda b, tok, be, nv: (layer, be[b], 0, 0)),
                  pl.BlockSpec((None, 1, F, D), lambda b, tok, be, nv: (layer, be[b], 0, 0))],
        out_specs=pl.BlockSpec((DISPATCH_BLOCK, D // 2), lambda b, tok, be, nv: (b, 0)),
        scratch_shapes=[pltpu.VMEM((GATHER_DEPTH, DISPATCH_BLOCK, D // 2), jnp.uint32),
                        pltpu.SemaphoreType.DMA((GATHER_DEPTH,))],
    )
    return pl.pallas_call(
        _expert_kernel,
        grid_spec=grid_spec,
        out_shape=jax.ShapeDtypeStruct((P, D // 2), jnp.uint32),
        compiler_params=_params("arbitrary"),
        name="moe_expert_ffn",
    )(slot_tok, blk_expert, n_valid, xp, w_gate, w_up, w_down)


COMBINE_TOKENS = 256


def _combine_kernel(dest_ref, ys_hbm, x_ref, gate_ref, g_ref, b_ref, of_ref, ob_ref, buf, sem,
                    *, n_tokens, tb, alpha):
    i = pl.program_id(0)

    def row_copy(half, k, r, src):
        return pltpu.make_async_copy(ys_hbm.at[pl.ds(src, 1)], buf.at[half, k, pl.ds(r, 1)], sem.at[half])

    def issue_rows(base, half, r0, count):
        for u in range(count):
            for k in range(TOP_K):
                row_copy(half, k, r0 + u, dest_ref[k * n_tokens + base + r0 + u]).start(priority=k % 2)

    def drain(half):
        def body(j, carry):
            for u in range(DMA_UNROLL * TOP_K):
                row_copy(half, 0, 0, 0).wait()
            return carry

        lax.fori_loop(0, tb // DMA_UNROLL, body, 0)

    last_step = pl.num_programs(0) - 1

    @pl.when(i == 0)
    def _():
        for ahead in range(GATHER_DEPTH - 1):
            base = jnp.minimum(ahead, last_step) * tb
            lax.fori_loop(0, tb // DMA_UNROLL,
                          lambda j, c: (issue_rows(base, ahead, j * DMA_UNROLL, DMA_UNROLL), c)[1], 0)

    half = i % GATHER_DEPTH
    drain(half)

    g = g_ref[...]
    b = b_ref[...]
    ahead_half = (i + GATHER_DEPTH - 1) % GATHER_DEPTH
    ahead_base = jnp.minimum(i + GATHER_DEPTH - 1, last_step) * tb

    def body(r, carry):
        issue_rows(ahead_base, ahead_half, r * LN_ROWS, LN_ROWS)
        rows = pl.ds(pl.multiple_of(r * LN_ROWS, LN_ROWS), LN_ROWS)
        gates = gate_ref[rows, :]
        m = _unpack_rows(buf[half, 0, rows, :]) * gates[:, 0:1]
        for k in range(1, TOP_K):
            m = m + _unpack_rows(buf[half, k, rows, :]) * gates[:, k:k + 1]
        out = _layer_norm_rows(alpha * x_ref[rows, :] + m, g, b)
        of_ref[rows, :] = out
        ob_ref[rows, :] = out.astype(BF16)
        return carry

    lax.fori_loop(0, tb // LN_ROWS, body, 0)

    @pl.when(i == last_step)
    def _():
        for ahead in range(1, GATHER_DEPTH):
            drain((i + ahead) % GATHER_DEPTH)


def _combine(ys, x, dest, gate_cols, g, b, alpha):
    L, D = x.shape
    tb = min(COMBINE_TOKENS, L)
    grid_spec = pltpu.PrefetchScalarGridSpec(
        num_scalar_prefetch=1,
        grid=(L // tb,),
        in_specs=[pl.BlockSpec(memory_space=pl.ANY),
                  pl.BlockSpec((tb, D), lambda i, d: (i, 0)),
                  pl.BlockSpec((tb, TOP_K), lambda i, d: (i, 0)),
                  pl.BlockSpec((1, D), lambda i, d: (0, 0)),
                  pl.BlockSpec((1, D), lambda i, d: (0, 0))],
        out_specs=[pl.BlockSpec((tb, D), lambda i, d: (i, 0)),
                   pl.BlockSpec((tb, D), lambda i, d: (i, 0))],
        scratch_shapes=[pltpu.VMEM((GATHER_DEPTH, TOP_K, tb, D // 2), ys.dtype),
                        pltpu.SemaphoreType.DMA((GATHER_DEPTH,))],
    )
    return pl.pallas_call(
        functools.partial(_combine_kernel, n_tokens=L, tb=tb, alpha=alpha),
        grid_spec=grid_spec,
        out_shape=[jax.ShapeDtypeStruct((L, D), F32), jax.ShapeDtypeStruct((L, D), BF16)],
        compiler_params=_params("arbitrary"),
        name="moe_combine_layernorm",
    )(dest, ys, x, gate_cols, g.reshape(1, D), b.reshape(1, D))


def _moe_layer(x, xp, router_w, router_b, w_gate, w_up, w_down, layer, g, b, alpha):
    L, D = x.shape
    blk = DISPATCH_BLOCK
    expert, gate = _route(x, router_w, router_b)
    rank, counts = _expert_ranks(expert)
    padded = (counts + blk - 1) // blk * blk
    pends = jnp.cumsum(padded)
    pstarts = pends - padded
    nb = (L * TOP_K) // blk + N_EXPERTS
    n_valid = (pends[-1] // blk).astype(jnp.int32)
    dest = (jnp.sum(jnp.where(expert[None] == jnp.arange(N_EXPERTS)[:, None, None],
                              pstarts[:, None, None], 0), axis=0) + rank).astype(jnp.int32)
    blk_idx = jnp.minimum(jnp.arange(nb, dtype=jnp.int32), n_valid - 1)
    blk_expert = jnp.minimum(jnp.sum(pends[None, :] <= (blk_idx * blk)[:, None], axis=1),
                             N_EXPERTS - 1).astype(jnp.int32)
    dest_flat = dest.reshape(TOP_K * L)
    slot_tok = _slot_tokens(dest_flat, L, nb * blk)
    ys = _expert_ffn(xp, slot_tok, w_gate, w_up, w_down, layer, blk_expert, n_valid.reshape(1))
    return _combine(ys, x, dest_flat, jnp.transpose(gate), g, b, alpha)


def kernel(x, gla_w_in, gla_w_gate2, gla_b_gate, gla_norm_g, gla_w_out, sgu_w_in, sgu_ln_g, sgu_ln_b,
           sgu_w_s, sgu_b_s, sgu_w_out, router_w, router_b, moe_w_gate, moe_w_up, moe_w_down, ln_g, ln_b):
    B, L, D = x.shape
    depth = ln_g.shape[0]
    alpha = float((2 * depth) ** 0.25)
    gla_w_out, sgu_w_in, sgu_w_out, moe_w_gate, moe_w_up, moe_w_down = (
        w.astype(BF16) for w in (gla_w_out, sgu_w_in, sgu_w_out, moe_w_gate, moe_w_up, moe_w_down))
    n_main = gla_w_in.shape[2] - 2 * GLA_GATE_RANK
    gla_w_main = gla_w_in[:, :, :n_main].astype(BF16)
    gla_w_z = jnp.pad(gla_w_in[:, :, n_main:].astype(BF16), ((0, 0), (0, 0), (0, LANES - 2 * GLA_GATE_RANK)))
    outs = []
    for bi in range(B):
        xf = x[bi]
        xb = xf.astype(BF16)
        for i in range(depth):
            j = i // 2
            if i % 2 == 0:
                a = _gla_mixer(xb, gla_w_main, gla_w_z, j, gla_w_gate2[j], gla_b_gate[j], gla_norm_g[j])
                w_out = gla_w_out
            else:
                zact = _matmul(xb, sgu_w_in, j, sgu_w_in.shape[2], BF16, act="gelu")
                a = _sgu_gate(zact, sgu_ln_g[j], sgu_ln_b[j], sgu_w_s[j], sgu_b_s[j])
                w_out = sgu_w_out
            xf, xp = _matmul_res_ln(a, w_out, j, xf, ln_g[i, 0], ln_b[i, 0], alpha)
            xf, xb = _moe_layer(xf, xp, router_w, router_b, moe_w_gate, moe_w_up, moe_w_down, i,
                                ln_g[i, 1], ln_b[i, 1], alpha)
        outs.append(xf)
    return jnp.stack(outs, axis=0)
```

```python
import functools

import jax
import jax.numpy as jnp
from jax import lax
from jax.experimental import pallas as pl
from jax.experimental.pallas import tpu as pltpu

GLA_HEADS = 8
GLA_GATE_RANK = 16
GLA_TAU = 16.0
GLA_CHUNK = 64
SGU_GROUPS = 8
SGU_CHUNK = 128
N_EXPERTS = 16
N_GROUPS = 4
EXPERTS_PER_GROUP = N_EXPERTS // N_GROUPS
TOP_K = 2
LN_EPS = 1e-5
RMS_EPS = 1e-6

LANES = 128
SUBLANES = 8
VMEM_LIMIT_BYTES = 56 * 1024 * 1024

DISPATCH_BLOCK = 256
NEG_BIG = -1e30

F32 = jnp.float32
BF16 = jnp.bfloat16
_NT = (((1,), (1,)), ((), ()))
_TN = (((0,), (0,)), ((), ()))


def _params(*sem):
    return pltpu.CompilerParams(dimension_semantics=sem, vmem_limit_bytes=VMEM_LIMIT_BYTES)


def _layer_norm_rows(y, g, b):
    mu = jnp.mean(y, axis=-1, keepdims=True)
    yc = y - mu
    var = jnp.mean(yc * yc, axis=-1, keepdims=True)
    return yc * lax.rsqrt(var + LN_EPS) * g + b


def _mm_kernel(x_ref, w_ref, o_ref, *, act):
    acc = jnp.dot(x_ref[...], w_ref[...], preferred_element_type=F32)
    if act == "gelu":
        acc = 0.5 * acc * (1.0 + lax.erf(acc * (2.0 ** -0.5)))
    o_ref[...] = acc.astype(o_ref.dtype)


def _tile(dim, target, align):
    t = min(target, dim) // align * align
    while dim % t:
        t -= align
    return t


def _matmul(x, w, layer, n, out_dtype, act=None, tm=1024, tn=1024):
    m, k = x.shape
    tm, tn = _tile(m, tm, SUBLANES), _tile(n, tn, LANES)
    return pl.pallas_call(
        functools.partial(_mm_kernel, act=act),
        grid=(m // tm, n // tn),
        in_specs=[pl.BlockSpec((tm, k), lambda i, j: (i, 0)),
                  pl.BlockSpec((None, k, tn), lambda i, j: (layer, 0, j))],
        out_specs=pl.BlockSpec((tm, tn), lambda i, j: (i, j)),
        out_shape=jax.ShapeDtypeStruct((m, n), out_dtype),
        compiler_params=_params("parallel", "arbitrary"),
        name="dense_matmul",
    )(x, w)


PACK_GROUP = 512
_HI_MASK = 0xFFFF0000


def _bf16_bits(x):
    return lax.bitcast_convert_type(x.astype(BF16).astype(F32), jnp.uint32)


def _pack_rows(x):
    half = PACK_GROUP // 2
    words = []
    for c0 in range(0, x.shape[1], PACK_GROUP):
        lo = _bf16_bits(x[:, c0:c0 + half]) >> 16
        hi = _bf16_bits(x[:, c0 + half:c0 + PACK_GROUP]) & jnp.uint32(_HI_MASK)
        words.append(hi | lo)
    return words[0] if len(words) == 1 else jnp.concatenate(words, axis=1)


def _unpack_rows(w):
    half = PACK_GROUP // 2
    cols = []
    for c0 in range(0, w.shape[1], half):
        piece = w[:, c0:c0 + half]
        cols.append(lax.bitcast_convert_type(piece << 16, F32))
        cols.append(lax.bitcast_convert_type(piece & jnp.uint32(_HI_MASK), F32))
    return jnp.concatenate(cols, axis=1)


LN_ROWS = 32


def _mm_ln_kernel(a_ref, w_ref, res_ref, g_ref, b_ref, of_ref, xp_ref, y_scr, mu_scr, rs_scr,
                  *, alpha, nj, tm, n):
    j = pl.program_id(1)

    @pl.when(j == 0)
    def _():
        mu_scr[...] = jnp.zeros_like(mu_scr)
        rs_scr[...] = jnp.zeros_like(rs_scr)

    @pl.when(j < nj)
    def _():
        y = alpha * res_ref[...] + jnp.dot(a_ref[...], w_ref[...], preferred_element_type=F32)
        y_scr[j] = y
        mu_scr[...] += jnp.sum(y, axis=-1, keepdims=True)
        rs_scr[...] += jnp.sum(y * y, axis=-1, keepdims=True)

    @pl.when(j == nj)
    def _():
        mu = mu_scr[...] * (1.0 / n)
        mu_scr[...] = mu
        rs_scr[...] = lax.rsqrt(rs_scr[...] * (1.0 / n) - mu * mu + LN_EPS)

    @pl.when(j >= nj)
    def _():
        t = j - nj
        g = g_ref[...]
        b = b_ref[...]

        def body(r, carry):
            rows = pl.ds(pl.multiple_of(r * LN_ROWS, LN_ROWS), LN_ROWS)
            out = (y_scr[t, rows, :] - mu_scr[rows, :]) * rs_scr[rows, :] * g + b
            of_ref[rows, :] = out
            xp_ref[rows, :] = _pack_rows(out)
            return carry

        lax.fori_loop(0, tm // LN_ROWS, body, 0, unroll=8)


def _matmul_res_ln(a, w, layer, res, g, b, alpha, tm=1024, tn=512):
    m, kdim = a.shape
    n = w.shape[2]
    tm, tn = _tile(m, tm, SUBLANES), _tile(n, tn, PACK_GROUP)
    nj = n // tn
    nb = m // tm
    fill_row = lambda i, j: jnp.where(j < nj, i, jnp.minimum(i + 1, nb - 1))
    fill_col = lambda j: jnp.where(j < nj, j, 0)
    emit = lambda j: jnp.maximum(j - nj, 0)
    return pl.pallas_call(
        functools.partial(_mm_ln_kernel, alpha=alpha, nj=nj, tm=tm, n=n),
        grid=(nb, 2 * nj),
        in_specs=[pl.BlockSpec((tm, kdim), lambda i, j: (fill_row(i, j), 0)),
                  pl.BlockSpec((None, kdim, tn), lambda i, j: (layer, 0, fill_col(j))),
                  pl.BlockSpec((tm, tn), lambda i, j: (fill_row(i, j), fill_col(j))),
                  pl.BlockSpec((1, tn), lambda i, j: (0, emit(j))),
                  pl.BlockSpec((1, tn), lambda i, j: (0, emit(j)))],
        out_specs=[pl.BlockSpec((tm, tn), lambda i, j: (i, emit(j))),
                   pl.BlockSpec((tm, tn // 2), lambda i, j: (i, emit(j)))],
        out_shape=[jax.ShapeDtypeStruct((m, n), F32),
                   jax.ShapeDtypeStruct((m, n // 2), jnp.uint32)],
        scratch_shapes=[pltpu.VMEM((nj, tm, tn), F32), pltpu.VMEM((tm, 1), F32), pltpu.VMEM((tm, 1), F32)],
        compiler_params=_params("parallel", "arbitrary"),
        name="matmul_residual_layernorm",
    )(a, w, res, g.reshape(1, n), b.reshape(1, n))


def _log_sigmoid(x):
    return jnp.minimum(x, 0.0) - jnp.log1p(jnp.exp(-jnp.abs(x)))


def _gla_kernel(*refs, reverse, finalize, n_chunks, scale):
    if finalize:
        (q_ref, k_ref, z_ref, qn_ref, kn_ref, zn_ref, v_ref, wg_ref, bg_ref, oprev_ref, r_ref, ng_ref,
         o_ref, g_scr, st_scr, oi_scr, u_scr, *prep) = refs
    else:
        (q_ref, k_ref, z_ref, qn_ref, kn_ref, zn_ref, v_ref, wg_ref, bg_ref,
         o_ref, g_scr, st_scr, oi_scr, u_scr, *prep) = refs
    set_a, set_b = prep[:4], prep[4:]
    C = GLA_CHUNK
    i = pl.program_id(1)

    row = lax.broadcasted_iota(jnp.int32, (C, C), 0)
    col = lax.broadcasted_iota(jnp.int32, (C, C), 1)
    if reverse:
        tri = jnp.where(col >= row, 1.0, 0.0).astype(BF16)
        mask = col > row
        last = 0
    else:
        tri = jnp.where(col <= row, 1.0, 0.0).astype(BF16)
        mask = col <= row
        last = C - 1

    chunk_rows = [slice(c * C, (c + 1) * C) for c in range(n_chunks)]

    def prepare_logits(zr):
        zh, zl = _split_bf16(zr[...])
        wh, wl = _split_bf16(wg_ref[...])
        return (jnp.dot(zh, wh, preferred_element_type=F32) + jnp.dot(zh, wl, preferred_element_type=F32)
                + jnp.dot(zl, wh, preferred_element_type=F32)) + bg_ref[...]

    def prepare_cumsums(logit):
        g_scr[...] = _log_sigmoid(logit) * (1.0 / GLA_TAU)
        sums = []
        for rows in chunk_rows:
            g_hi, g_lo = _split_bf16(g_scr[rows, :])
            sums.append(jnp.dot(tri, g_hi, preferred_element_type=F32)
                        + jnp.dot(tri, g_lo, preferred_element_type=F32))
        return sums

    def prepare_store(qr, kr, sums, dst):
        qd_scr, kd_scr, ks_scr, dec_scr = dst
        for c, rows in enumerate(chunk_rows):
            b = sums[c]
            b_last = b[last:last + 1, :]
            q = qr[rows, :].astype(F32) * scale
            k = kr[rows, :].astype(F32)
            qd_scr[rows, :] = (q * jnp.exp(b)).astype(BF16)
            kd_scr[rows, :] = (k * jnp.exp(-b)).astype(BF16)
            ks_scr[rows, :] = (k * jnp.exp(b_last - b)).astype(BF16)
            dec_scr[c:c + 1, :] = jnp.exp(b_last)

    def attend_scores(src):
        qd_scr, kd_scr, ks_scr, _ = src
        scores = [lax.dot_general(qd_scr[rows, :], kd_scr[rows, :], _NT, preferred_element_type=F32)
                  for rows in chunk_rows]
        for c, rows in enumerate(chunk_rows):
            u_scr[c] = lax.dot_general(v_ref[rows, :], ks_scr[rows, :], _TN, preferred_element_type=F32)
        return scores

    def attend_outputs(src, scores):
        qd_scr, _, _, dec_scr = src
        for c, rows in enumerate(chunk_rows):
            masked = jnp.where(mask, scores[c], 0.0).astype(BF16)
            oi_scr[rows, :] = jnp.dot(masked, v_ref[rows, :], preferred_element_type=F32)
        st = st_scr[...]
        for c in (reversed(range(n_chunks)) if reverse else range(n_chunks)):
            rows = chunk_rows[c]
            o = oi_scr[rows, :] + lax.dot_general(qd_scr[rows, :], st.astype(BF16), _NT,
                                                  preferred_element_type=F32)
            st = dec_scr[c:c + 1, :] * st + u_scr[c]
            if finalize:
                o = o + oprev_ref[rows, :].astype(F32)
                o = o * lax.rsqrt(jnp.mean(o * o, axis=-1, keepdims=True) + RMS_EPS)
                r = r_ref[rows, :].astype(F32)
                o = o * ng_ref[...] * (r * jax.nn.sigmoid(r))
            o_ref[rows, :] = o.astype(o_ref.dtype)
        st_scr[...] = st

    def step(cur, nxt):
        logit = prepare_logits(zn_ref)
        scores = attend_scores(cur)
        sums = prepare_cumsums(logit)
        attend_outputs(cur, scores)
        prepare_store(qn_ref, kn_ref, sums, nxt)

    @pl.when(i == 0)
    def _():
        st_scr[...] = jnp.zeros_like(st_scr)
        prepare_store(q_ref, k_ref, prepare_cumsums(prepare_logits(z_ref)), set_a)

    @pl.when(i % 2 == 0)
    def _():
        step(set_a, set_b)

    @pl.when(i % 2 == 1)
    def _():
        step(set_b, set_a)


def _gla_direction(proj, z, wg_pad, b_gate, *, reverse, o_prev=None, norm_g=None, tl=1024):
    L = proj.shape[0]
    H = GLA_HEADS
    dk_total = wg_pad.shape[1]
    dk = dk_total // H
    dv_total = (proj.shape[1] - 2 * dk_total) // 2
    dv = dv_total // H
    tl = min(tl, L)
    nb = L // tl
    n_chunks = tl // GLA_CHUNK
    finalize = o_prev is not None
    blk = (lambda i: nb - 1 - i) if reverse else (lambda i: i)
    nxt = lambda i: blk(jnp.minimum(i + 1, nb - 1))
    k_off = dk_total // dk
    v_off = 2 * dk_total // dv
    r_off = v_off + dv_total // dv

    in_specs = [
        pl.BlockSpec((tl, dk), lambda h, i: (blk(i), h)),
        pl.BlockSpec((tl, dk), lambda h, i: (blk(i), k_off + h)),
        pl.BlockSpec((tl, LANES), lambda h, i: (blk(i), 0)),
        pl.BlockSpec((tl, dk), lambda h, i: (nxt(i), h)),
        pl.BlockSpec((tl, dk), lambda h, i: (nxt(i), k_off + h)),
        pl.BlockSpec((tl, LANES), lambda h, i: (nxt(i), 0)),
        pl.BlockSpec((tl, dv), lambda h, i: (blk(i), v_off + h)),
        pl.BlockSpec((LANES, dk), lambda h, i: (0, h)),
        pl.BlockSpec((1, dk), lambda h, i: (0, h)),
    ]
    args = [proj, proj, z, proj, proj, z, proj, wg_pad, b_gate.reshape(1, dk_total)]
    if finalize:
        in_specs += [
            pl.BlockSpec((tl, dv), lambda h, i: (blk(i), h)),
            pl.BlockSpec((tl, dv), lambda h, i: (blk(i), r_off + h)),
            pl.BlockSpec((1, dv), lambda h, i: (0, h)),
        ]
        args += [o_prev, proj, norm_g.reshape(1, dv_total)]
    prep_set = [pltpu.VMEM((tl, dk), BF16), pltpu.VMEM((tl, dk), BF16), pltpu.VMEM((tl, dk), BF16),
                pltpu.VMEM((n_chunks, dk), F32)]
    return pl.pallas_call(
        functools.partial(_gla_kernel, reverse=reverse, finalize=finalize,
                          n_chunks=n_chunks, scale=float(dk) ** -0.5),
        grid=(H, nb),
        in_specs=in_specs,
        out_specs=pl.BlockSpec((tl, dv), lambda h, i: (blk(i), h)),
        out_shape=jax.ShapeDtypeStruct((L, dv_total), BF16),
        scratch_shapes=[pltpu.VMEM((tl, dk), F32), pltpu.VMEM((dv, dk), F32),
                        pltpu.VMEM((tl, dv), F32), pltpu.VMEM((n_chunks, dv, dk), F32)] + prep_set + prep_set,
        compiler_params=_params("parallel", "arbitrary"),
        name="gla_backward_scan" if reverse else "gla_forward_scan",
    )(*args)


def _gla_mixer(xb, w_in, w_z, layer, w_gate2, b_gate, norm_g):
    dk_total = w_gate2.shape[-1]
    rank = GLA_GATE_RANK
    proj = _matmul(xb, w_in, layer, w_in.shape[2] - 2 * rank, BF16, tn=1536)
    z = _matmul(xb, w_z, layer, LANES, F32)
    wg_f = jnp.zeros((LANES, dk_total), F32).at[:rank].set(w_gate2[0])
    wg_b = jnp.zeros((LANES, dk_total), F32).at[rank:2 * rank].set(w_gate2[1])
    o_f = _gla_direction(proj, z, wg_f, b_gate[0], reverse=False)
    return _gla_direction(proj, z, wg_b, b_gate[1], reverse=True, o_prev=o_f, norm_g=norm_g)


def _sgu_kernel(u_ref, v_ref, g_ref, b_ref, ws_ref, bias_ref, o_ref, *, n_chunks, gw):
    C = SGU_CHUNK
    for c in range(n_chunks):
        rows = slice(c * C, (c + 1) * C)
        vn = _layer_norm_rows(v_ref[rows, :].astype(F32), g_ref[...], b_ref[...]).astype(BF16)
        for grp in range(SGU_GROUPS):
            cols = slice(grp * gw, (grp + 1) * gw)
            mixed = jnp.dot(ws_ref[grp], vn[:, cols], preferred_element_type=F32) + bias_ref[:, cols]
            o_ref[rows, cols] = (u_ref[rows, cols].astype(F32) * mixed).astype(BF16)


def _sgu_gate(zact, ln_g, ln_b, w_s, b_s, tl=256):
    L = zact.shape[0]
    half = zact.shape[1] // 2
    gw = half // SGU_GROUPS
    tl = min(tl, L)
    bias = jnp.repeat(jnp.transpose(b_s), gw, axis=1)
    return pl.pallas_call(
        functools.partial(_sgu_kernel, n_chunks=tl // SGU_CHUNK, gw=gw),
        grid=(L // tl,),
        in_specs=[pl.BlockSpec((tl, half), lambda i: (i, 0)),
                  pl.BlockSpec((tl, half), lambda i: (i, 1)),
                  pl.BlockSpec((1, half), lambda i: (0, 0)),
                  pl.BlockSpec((1, half), lambda i: (0, 0)),
                  pl.BlockSpec((SGU_GROUPS, SGU_CHUNK, SGU_CHUNK), lambda i: (0, 0, 0)),
                  pl.BlockSpec((SGU_CHUNK, half), lambda i: (0, 0))],
        out_specs=pl.BlockSpec((tl, half), lambda i: (i, 0)),
        out_shape=jax.ShapeDtypeStruct((L, half), BF16),
        compiler_params=_params("parallel"),
        name="sgu_spatial_gate",
    )(zact, zact, ln_g.reshape(1, half), ln_b.reshape(1, half), w_s.astype(BF16), bias)


ROUTER_ROWS = EXPERTS_PER_GROUP * SUBLANES


def _split_bf16(a):
    hi = a.astype(BF16)
    lo = (a - hi.astype(F32)).astype(BF16)
    return hi, lo


def _router_kernel(x_ref, wt_ref, rb_ref, o_ref):
    xh, xl = _split_bf16(x_ref[...])
    wh, wl = _split_bf16(wt_ref[...])
    logits = (lax.dot_general(wh, xh, _NT, preferred_element_type=F32)
              + lax.dot_general(wh, xl, _NT, preferred_element_type=F32)
              + lax.dot_general(wl, xh, _NT, preferred_element_type=F32))
    scores = jax.nn.sigmoid(logits)
    sel = scores + rb_ref[:, 0:1]
    T = logits.shape[1]
    s_m = [scores[SUBLANES * m:SUBLANES * (m + 1), :] for m in range(EXPERTS_PER_GROUP)]
    a_m = [sel[SUBLANES * m:SUBLANES * (m + 1), :] for m in range(EXPERTS_PER_GROUP)]
    hi1, lo1 = jnp.maximum(a_m[0], a_m[1]), jnp.minimum(a_m[0], a_m[1])
    hi2, lo2 = jnp.maximum(a_m[2], a_m[3]), jnp.minimum(a_m[2], a_m[3])
    group_score = jnp.maximum(hi1, hi2) + jnp.maximum(jnp.minimum(hi1, hi2), jnp.maximum(lo1, lo2))
    gid = lax.broadcasted_iota(jnp.int32, (SUBLANES, T), 0)
    gmax = jnp.max(group_score, axis=0, keepdims=True)
    g_star = jnp.min(jnp.where(group_score == gmax, gid, SUBLANES), axis=0, keepdims=True)
    onehot = gid == g_star
    v_m = [jnp.sum(jnp.where(onehot, a, 0.0), axis=0, keepdims=True) for a in a_m]
    r_m = [jnp.sum(jnp.where(onehot, s, 0.0), axis=0, keepdims=True) for s in s_m]
    best1, idx1, raw1 = v_m[0], jnp.zeros_like(g_star), r_m[0]
    for m in range(1, EXPERTS_PER_GROUP):
        take = v_m[m] > best1
        best1 = jnp.where(take, v_m[m], best1)
        idx1 = jnp.where(take, m, idx1)
        raw1 = jnp.where(take, r_m[m], raw1)
    best2 = jnp.full_like(best1, -jnp.inf)
    idx2, raw2 = jnp.zeros_like(g_star), jnp.zeros_like(raw1)
    for m in range(EXPERTS_PER_GROUP):
        take = jnp.logical_and(idx1 != m, v_m[m] > best2)
        best2 = jnp.where(take, v_m[m], best2)
        idx2 = jnp.where(take, m, idx2)
        raw2 = jnp.where(take, r_m[m], raw2)
    denom = raw1 + raw2
    e1 = (g_star * EXPERTS_PER_GROUP + idx1).astype(F32)
    e2 = (g_star * EXPERTS_PER_GROUP + idx2).astype(F32)
    rid = lax.broadcasted_iota(jnp.int32, (SUBLANES, T), 0)
    out = jnp.where(rid == 0, e1, jnp.where(rid == 1, e2, jnp.where(
        rid == 2, raw1 / denom, jnp.where(rid == 3, raw2 / denom, 0.0))))
    o_ref[...] = out


def _route(x, router_w, router_b, tl=512):
    L, D = x.shape
    tl = min(tl, L)
    w_t = jnp.zeros((EXPERTS_PER_GROUP, SUBLANES, D), F32)
    w_t = w_t.at[:, :N_GROUPS].set(router_w.T.reshape(N_GROUPS, EXPERTS_PER_GROUP, D).transpose(1, 0, 2))
    rb = jnp.full((EXPERTS_PER_GROUP, SUBLANES), NEG_BIG, F32)
    rb = rb.at[:, :N_GROUPS].set(router_b.astype(F32).reshape(N_GROUPS, EXPERTS_PER_GROUP).T)
    rb = jnp.broadcast_to(rb.reshape(ROUTER_ROWS, 1), (ROUTER_ROWS, LANES))
    out = pl.pallas_call(
        _router_kernel,
        grid=(L // tl,),
        in_specs=[pl.BlockSpec((tl, D), lambda i: (i, 0)),
                  pl.BlockSpec((ROUTER_ROWS, D), lambda i: (0, 0)),
                  pl.BlockSpec((ROUTER_ROWS, LANES), lambda i: (0, 0))],
        out_specs=pl.BlockSpec((SUBLANES, tl), lambda i: (0, i)),
        out_shape=jax.ShapeDtypeStruct((SUBLANES, L), F32),
        compiler_params=_params("parallel"),
        name="moe_router",
    )(x, w_t.reshape(ROUTER_ROWS, D), rb)
    expert = out[:TOP_K].astype(jnp.int32)
    gate = out[TOP_K:2 * TOP_K]
    return expert, gate


def _rank_kernel(e_ref, rank_ref, cnt_ref, carry_scr):
    @pl.when(jnp.logical_and(pl.program_id(0) == 0, pl.program_id(1) == 0))
    def _():
        carry_scr[...] = jnp.zeros_like(carry_scr)

    e = e_ref[0]
    T = e.shape[1]
    eid = lax.broadcasted_iota(jnp.int32, (N_EXPERTS, T), 0)
    onehot = eid == e
    oh = jnp.where(onehot, 1.0, 0.0)
    s_idx = lax.broadcasted_iota(jnp.int32, (T, T), 0)
    t_idx = lax.broadcasted_iota(jnp.int32, (T, T), 1)
    before = jnp.where(s_idx < t_idx, 1.0, 0.0).astype(BF16)
    prefix = jnp.dot(oh.astype(BF16), before, preferred_element_type=F32)
    carry = carry_scr[:, 0:1]
    rank_ref[0] = jnp.sum(jnp.where(onehot, prefix + carry, 0.0), axis=0, keepdims=True)
    carry_scr[...] = carry_scr[...] + jnp.sum(oh, axis=1, keepdims=True)
    cnt_ref[...] = carry_scr[...]


def _expert_ranks(expert, tl=512):
    K, L = expert.shape
    tl = min(tl, L)
    rank, cnt = pl.pallas_call(
        _rank_kernel,
        grid=(K, L // tl),
        in_specs=[pl.BlockSpec((1, 1, tl), lambda k, i: (k, 0, i))],
        out_specs=[pl.BlockSpec((1, 1, tl), lambda k, i: (k, 0, i)),
                   pl.BlockSpec((N_EXPERTS, LANES), lambda k, i: (0, 0))],
        out_shape=[jax.ShapeDtypeStruct((K, 1, L), F32),
                   jax.ShapeDtypeStruct((N_EXPERTS, LANES), F32)],
        scratch_shapes=[pltpu.VMEM((N_EXPERTS, LANES), F32)],
        compiler_params=_params("arbitrary", "arbitrary"),
        name="moe_expert_rank",
    )(expert.reshape(K, 1, L))
    return rank.reshape(K, L).astype(jnp.int32), cnt[:, 0].astype(jnp.int32)


DMA_UNROLL = 8
GATHER_DEPTH = 3


def _slot_token_kernel(dest_ref, tok_ref, *, n_tokens, n_slots):
    def clear(j, carry):
        for u in range(DMA_UNROLL):
            tok_ref[j * DMA_UNROLL + u] = 0
        return carry

    lax.fori_loop(0, n_slots // DMA_UNROLL, clear, 0)
    for k in range(TOP_K):
        def scatter(j, carry):
            for u in range(DMA_UNROLL):
                t = j * DMA_UNROLL + u
                tok_ref[dest_ref[k * n_tokens + t]] = t
            return carry

        lax.fori_loop(0, n_tokens // DMA_UNROLL, scatter, 0)


def _slot_tokens(dest, n_tokens, n_slots):
    return pl.pallas_call(
        functools.partial(_slot_token_kernel, n_tokens=n_tokens, n_slots=n_slots),
        in_specs=[pl.BlockSpec(memory_space=pltpu.SMEM)],
        out_specs=pl.BlockSpec(memory_space=pltpu.SMEM),
        out_shape=jax.ShapeDtypeStruct((n_slots,), jnp.int32),
        name="moe_slot_tokens",
    )(dest)


def _expert_kernel(tok_ref, bexp_ref, nvalid_ref, x_hbm, wg_ref, wu_ref, wd_ref, o_ref, xbuf, sem):
    b = pl.program_id(0)
    n_valid = nvalid_ref[0]
    rows = xbuf.shape[1]

    def row_copy(slot, r, tok):
        return pltpu.make_async_copy(x_hbm.at[pl.ds(tok, 1)], xbuf.at[slot, pl.ds(r, 1)], sem.at[slot])

    def drain(slot):
        def body(j, carry):
            for u in range(DMA_UNROLL):
                row_copy(slot, 0, 0).wait()
            return carry

        lax.fori_loop(0, rows // DMA_UNROLL, body, 0)

    last_blk = n_valid - 1

    @pl.when(b == 0)
    def _():
        for ahead in range(GATHER_DEPTH - 1):
            base = jnp.minimum(ahead, last_blk) * rows

            def issue(j, carry):
                for u in range(DMA_UNROLL):
                    r = j * DMA_UNROLL + u
                    row_copy(ahead, r, tok_ref[base + r]).start(priority=u % 2)
                return carry

            lax.fori_loop(0, rows // DMA_UNROLL, issue, 0)

    @pl.when(b < n_valid)
    def _():
        slot = b % GATHER_DEPTH
        drain(slot)
        ahead_slot = (b + GATHER_DEPTH - 1) % GATHER_DEPTH
        base = jnp.minimum(b + GATHER_DEPTH - 1, last_blk) * rows
        for r in range(rows):
            row_copy(ahead_slot, r, tok_ref[base + r]).start(priority=r % 2)
        x = _unpack_rows(xbuf[slot]).astype(BF16)
        hg = jnp.dot(x, wg_ref[0], preferred_element_type=F32)
        hu = jnp.dot(x, wu_ref[0], preferred_element_type=F32)
        a = (hg * jax.nn.sigmoid(hg) * hu).astype(BF16)
        o_ref[...] = _pack_rows(jnp.dot(a, wd_ref[0], preferred_element_type=F32))

    @pl.when(b == last_blk)
    def _():
        for ahead in range(1, GATHER_DEPTH):
            drain((b + ahead) % GATHER_DEPTH)

    @pl.when(b >= n_valid)
    def _():
        o_ref[...] = jnp.zeros_like(o_ref)


def _expert_ffn(xp, slot_tok, w_gate, w_up, w_down, layer, blk_expert, n_valid):
    D, F = w_gate.shape[2:]
    P = slot_tok.shape[0]
    nb = P // DISPATCH_BLOCK
    grid_spec = pltpu.PrefetchScalarGridSpec(
        num_scalar_prefetch=3,
        grid=(nb,),
        in_specs=[pl.BlockSpec(memory_space=pl.ANY),
                  pl.BlockSpec((None, 1, D, F), lambda b, tok, be, nv: (layer, be[b], 0, 0)),
                  pl.BlockSpec((None, 1, D, F), lambda b, tok, be, nv: (layer, be[b], 0, 0)),
                  pl.BlockSpec((None, 1, F, D), lambda b, tok, be, nv: (layer, be[b], 0, 0))],
        out_specs=pl.BlockSpec((DISPATCH_BLOCK, D // 2), lambda b, tok, be, nv: (b, 0)),
        scratch_shapes=[pltpu.VMEM((GATHER_DEPTH, DISPATCH_BLOCK, D // 2), jnp.uint32),
                        pltpu.SemaphoreType.DMA((GATHER_DEPTH,))],
    )
    return pl.pallas_call(
        _expert_kernel,
        grid_spec=grid_spec,
        out_shape=jax.ShapeDtypeStruct((P, D // 2), jnp.uint32),
        compiler_params=_params("arbitrary"),
        name="moe_expert_ffn",
    )(slot_tok, blk_expert, n_valid, xp, w_gate, w_up, w_down)


COMBINE_TOKENS = 256


def _combine_kernel(dest_ref, ys_hbm, x_ref, gate_ref, g_ref, b_ref, of_ref, ob_ref, buf, sem,
                    *, n_tokens, tb, alpha):
    i = pl.program_id(0)

    def row_copy(half, k, r, src):
        return pltpu.make_async_copy(ys_hbm.at[pl.ds(src, 1)], buf.at[half, k, pl.ds(r, 1)], sem.at[half])

    def issue_rows(base, half, r0, count):
        for u in range(count):
            for k in range(TOP_K):
                row_copy(half, k, r0 + u, dest_ref[k * n_tokens + base + r0 + u]).start(priority=k % 2)

    def drain(half):
        def body(j, carry):
            for u in range(DMA_UNROLL * TOP_K):
                row_copy(half, 0, 0, 0).wait()
            return carry

        lax.fori_loop(0, tb // DMA_UNROLL, body, 0)

    last_step = pl.num_programs(0) - 1

    @pl.when(i == 0)
    def _():
        for ahead in range(GATHER_DEPTH - 1):
            base = jnp.minimum(ahead, last_step) * tb
            lax.fori_loop(0, tb // DMA_UNROLL,
                          lambda j, c: (issue_rows(base, ahead, j * DMA_UNROLL, DMA_UNROLL), c)[1], 0)

    half = i % GATHER_DEPTH
    drain(half)

    g = g_ref[...]
    b = b_ref[...]
    ahead_half = (i + GATHER_DEPTH - 1) % GATHER_DEPTH
    ahead_base = jnp.minimum(i + GATHER_DEPTH - 1, last_step) * tb

    def body(r, carry):
        issue_rows(ahead_base, ahead_half, r * LN_ROWS, LN_ROWS)
        rows = pl.ds(pl.multiple_of(r * LN_ROWS, LN_ROWS), LN_ROWS)
        gates = gate_ref[rows, :]
        m = _unpack_rows(buf[half, 0, rows, :]) * gates[:, 0:1]
        for k in range(1, TOP_K):
            m = m + _unpack_rows(buf[half, k, rows, :]) * gates[:, k:k + 1]
        out = _layer_norm_rows(alpha * x_ref[rows, :] + m, g, b)
        of_ref[rows, :] = out
        ob_ref[rows, :] = out.astype(BF16)
        return carry

    lax.fori_loop(0, tb // LN_ROWS, body, 0)

    @pl.when(i == last_step)
    def _():
        for ahead in range(1, GATHER_DEPTH):
            drain((i + ahead) % GATHER_DEPTH)


def _combine(ys, x, dest, gate_cols, g, b, alpha):
    L, D = x.shape
    tb = min(COMBINE_TOKENS, L)
    grid_spec = pltpu.PrefetchScalarGridSpec(
        num_scalar_prefetch=1,
        grid=(L // tb,),
        in_specs=[pl.BlockSpec(memory_space=pl.ANY),
                  pl.BlockSpec((tb, D), lambda i, d: (i, 0)),
                  pl.BlockSpec((tb, TOP_K), lambda i, d: (i, 0)),
                  pl.BlockSpec((1, D), lambda i, d: (0, 0)),
                  pl.BlockSpec((1, D), lambda i, d: (0, 0))],
        out_specs=[pl.BlockSpec((tb, D), lambda i, d: (i, 0)),
                   pl.BlockSpec((tb, D), lambda i, d: (i, 0))],
        scratch_shapes=[pltpu.VMEM((GATHER_DEPTH, TOP_K, tb, D // 2), ys.dtype),
                        pltpu.SemaphoreType.DMA((GATHER_DEPTH,))],
    )
    return pl.pallas_call(
        functools.partial(_combine_kernel, n_tokens=L, tb=tb, alpha=alpha),
        grid_spec=grid_spec,
        out_shape=[jax.ShapeDtypeStruct((L, D), F32), jax.ShapeDtypeStruct((L, D), BF16)],
        compiler_params=_params("arbitrary"),
        name="moe_combine_layernorm",
    )(dest, ys, x, gate_cols, g.reshape(1, D), b.reshape(1, D))


def _moe_layer(x, xp, router_w, router_b, w_gate, w_up, w_down, layer, g, b, alpha):
    L, D = x.shape
    blk = DISPATCH_BLOCK
    expert, gate = _route(x, router_w, router_b)
    rank, counts = _expert_ranks(expert)
    padded = (counts + blk - 1) // blk * blk
    pends = jnp.cumsum(padded)
    pstarts = pends - padded
    nb = (L * TOP_K) // blk + N_EXPERTS
    n_valid = (pends[-1] // blk).astype(jnp.int32)
    dest = (jnp.sum(jnp.where(expert[None] == jnp.arange(N_EXPERTS)[:, None, None],
                              pstarts[:, None, None], 0), axis=0) + rank).astype(jnp.int32)
    blk_idx = jnp.minimum(jnp.arange(nb, dtype=jnp.int32), n_valid - 1)
    blk_expert = jnp.minimum(jnp.sum(pends[None, :] <= (blk_idx * blk)[:, None], axis=1),
                             N_EXPERTS - 1).astype(jnp.int32)
    dest_flat = dest.reshape(TOP_K * L)
    slot_tok = _slot_tokens(dest_flat, L, nb * blk)
    ys = _expert_ffn(xp, slot_tok, w_gate, w_up, w_down, layer, blk_expert, n_valid.reshape(1))
    return _combine(ys, x, dest_flat, jnp.transpose(gate), g, b, alpha)


def kernel(x, gla_w_in, gla_w_gate2, gla_b_gate, gla_norm_g, gla_w_out, sgu_w_in, sgu_ln_g, sgu_ln_b,
           sgu_w_s, sgu_b_s, sgu_w_out, router_w, router_b, moe_w_gate, moe_w_up, moe_w_down, ln_g, ln_b):
    B, L, D = x.shape
    depth = ln_g.shape[0]
    alpha = float((2 * depth) ** 0.25)
    gla_w_in, gla_w_out, sgu_w_in, sgu_w_out, moe_w_gate, moe_w_up, moe_w_down = (
        w.astype(BF16) for w in (gla_w_in, gla_w_out, sgu_w_in, sgu_w_out, moe_w_gate, moe_w_up, moe_w_down))
    n_main = gla_w_in.shape[2] - 2 * GLA_GATE_RANK
    gla_w_z = jnp.pad(gla_w_in[:, :, n_main:], ((0, 0), (0, 0), (0, LANES - 2 * GLA_GATE_RANK)))
    outs = []
    for bi in range(B):
        xf = x[bi]
        xb = xf.astype(BF16)
        for i in range(depth):
            j = i // 2
            if i % 2 == 0:
                a = _gla_mixer(xb, gla_w_in, gla_w_z, j, gla_w_gate2[j], gla_b_gate[j], gla_norm_g[j])
                w_out = gla_w_out
            else:
                zact = _matmul(xb, sgu_w_in, j, sgu_w_in.shape[2], BF16, act="gelu")
                a = _sgu_gate(zact, sgu_ln_g[j], sgu_ln_b[j], sgu_w_s[j], sgu_b_s[j])
                w_out = sgu_w_out
            xf, xp = _matmul_res_ln(a, w_out, j, xf, ln_g[i, 0], ln_b[i, 0], alpha)
            xf, xb = _moe_layer(xf, xp, router_w, router_b, moe_w_gate, moe_w_up, moe_w_down, i,
                                ln_g[i, 1], ln_b[i, 1], alpha)
        outs.append(xf)
    return jnp.stack(outs, axis=0)
```
